```python
import math
import jax, jax.numpy as jnp
from jax import lax
import numpy as np

D_MODEL = 2048
BATCH = 2
SEQ = 8192
DEPTH = 4

GRID_W = 64
CTX_LEN = 256
N_MOD = 6
ATT_HEADS = 8
ATT_HEAD_DIM = 64
ATT_V_DIM = 2 * ATT_HEAD_DIM
QK_WIDTH = ATT_HEADS * 2 * ATT_HEAD_DIM
ATT_WIDTH = ATT_HEADS * ATT_V_DIM
Q_BLOCK = 128
ROPE_BASE = 10000.0
AXIS_ROT = ATT_HEAD_DIM // 2
POOL_WINDOWS = (2, 4, 8, 16)
POOL_GROUPS = len(POOL_WINDOWS)
POOL_WIDTH = D_MODEL // 4
POOL_GROUP_DIM = POOL_WIDTH // POOL_GROUPS
CONV_WIDTH = D_MODEL // 4
CONV_K = 3
MIX_WIDTH = ATT_WIDTH + POOL_WIDTH + CONV_WIDTH
IN_WIDTH = 2 * QK_WIDTH + ATT_WIDTH + POOL_WIDTH + 3 * CONV_WIDTH
SPLIT_POINTS = (QK_WIDTH, 2 * QK_WIDTH, 2 * QK_WIDTH + ATT_WIDTH,
                2 * QK_WIDTH + ATT_WIDTH + POOL_WIDTH,
                2 * QK_WIDTH + ATT_WIDTH + POOL_WIDTH + CONV_WIDTH,
                2 * QK_WIDTH + ATT_WIDTH + POOL_WIDTH + 2 * CONV_WIDTH)
N_EXPERTS = 16
EXPERT_FF = 1024
CAP_FACTOR = 2
EPS = 1e-6

kernel_name = 'hybrid_parallel_group_dit_block'


def rmsnorm(x, g):
    x32 = x.astype(jnp.float32)
    y = x32 * lax.rsqrt(jnp.mean(x32 * x32, axis=-1, keepdims=True) + EPS)
    return y.astype(x.dtype) * g


def modulate(h, shift, scale):
    return h * (1.0 + scale) + shift


def axial_rope_tables(row, col, dtype):
    inv = 1.0 / (ROPE_BASE ** (jnp.arange(0, AXIS_ROT, 2, dtype=jnp.float32) / AXIS_ROT))
    ang = jnp.concatenate([row[:, None].astype(jnp.float32) * inv,
                           col[:, None].astype(jnp.float32) * inv], axis=-1)
    return jnp.cos(ang).astype(dtype), jnp.sin(ang).astype(dtype)


def apply_rope(x, cos, sin):
    c = cos[None, :, None, None, :]
    s = sin[None, :, None, None, :]
    x1, x2 = x[..., :ATT_HEAD_DIM // 2], x[..., ATT_HEAD_DIM // 2:]
    return jnp.concatenate([x1 * c - x2 * s, x1 * s + x2 * c], axis=-1)


def split_groups(u):
    B, n, _ = u.shape
    q, k, v, p, gb, gc, xin = jnp.split(u, SPLIT_POINTS, axis=-1)
    q = q.reshape(B, n, ATT_HEADS, 2, ATT_HEAD_DIM)
    k = k.reshape(B, n, ATT_HEADS, 2, ATT_HEAD_DIM)
    v = v.reshape(B, n, ATT_HEADS, ATT_V_DIM)
    return q, k, v, p, gb, gc, xin


def diff_attn_core(q, k, v, lam):
    s = jnp.einsum('bqhmd,bkhmd->bhmqk', q, k).astype(jnp.float32) * (ATT_HEAD_DIM ** -0.5)
    a = jax.nn.softmax(s, axis=-1)
    w = (a[:, :, 0] - lam * a[:, :, 1]).astype(v.dtype)
    return jnp.einsum('bhqk,bkhe->bqhe', w, v)


def diff_attn_blocked(q, k, v, lam):
    B, n = q.shape[:2]
    nb = n // Q_BLOCK
    qb = q.reshape(B, nb, Q_BLOCK, ATT_HEADS, 2, ATT_HEAD_DIM).transpose(1, 0, 2, 3, 4, 5)
    o = lax.map(lambda qblk: diff_attn_core(qblk, k, v, lam), qb)
    return o.transpose(1, 0, 2, 3, 4).reshape(B, n, ATT_HEADS, ATT_V_DIM)


def pool_mix(p, pool_w, pool_scale):
    B, n, _ = p.shape
    p32 = p.astype(jnp.float32)
    cs = jnp.concatenate([jnp.zeros((B, 1, POOL_WIDTH), jnp.float32), jnp.cumsum(p32, axis=1)], axis=1)
    t = jnp.arange(n)
    feats = []
    for g, w in enumerate(POOL_WINDOWS):
        lo = jnp.clip(t - w // 2, 0, n)
        hi = jnp.clip(t + w - w // 2, 0, n)
        sl = slice(g * POOL_GROUP_DIM, (g + 1) * POOL_GROUP_DIM)
        cnt = (hi - lo).astype(jnp.float32)[None, :, None]
        feats.append((cs[:, hi, sl] - cs[:, lo, sl]) / cnt - p32[..., sl])
    f = jnp.stack(feats, axis=2).astype(p.dtype)
    y = jnp.einsum('bngc,gcd->bngd', f, pool_w).reshape(B, n, POOL_WIDTH)
    return y * pool_scale


def conv_mix(gb, gc, xin, conv_w):
    z = gc * xin
    zp = jnp.pad(z, ((0, 0), (1, 1), (0, 0)))
    y = zp[:, :-2] * conv_w[0] + zp[:, 1:-1] * conv_w[1] + zp[:, 2:] * conv_w[2]
    return gb * y


def mixer_output(att, p, gb, gc, xin, subln_g, lam_init, pool_w, pool_scale, conv_w, w_out):
    B, n = att.shape[:2]
    a = (rmsnorm(att, subln_g) * (1.0 - lam_init)).reshape(B, n, ATT_WIDTH)
    y = jnp.concatenate([a, pool_mix(p, pool_w, pool_scale), conv_mix(gb, gc, xin, conv_w)], axis=-1)
    return y @ w_out


def expert_choice_ffn(h, w_router, w_gate, w_up, w_down):
    B, n, D = h.shape
    cap = CAP_FACTOR * n // N_EXPERTS
    aff = jax.nn.softmax((h @ w_router).astype(jnp.float32), axis=-1)
    g, idx = lax.top_k(aff.transpose(0, 2, 1), cap)
    xg = jax.vmap(lambda hb, ib: hb[ib])(h, idx)
    a = jnp.einsum('becd,edf->becf', xg, w_gate)
    u = jnp.einsum('becd,edf->becf', xg, w_up)
    y = jnp.einsum('becf,efd->becd', jax.nn.silu(a) * u, w_down) * g[..., None].astype(h.dtype)
    return jax.vmap(lambda ib, yb: jnp.zeros((n, D), yb.dtype).at[ib.reshape(-1)].add(yb.reshape(-1, D)))(idx, y)


def setup_inputs(seed: int = 0) -> dict:
    key = jax.random.key(seed)
    ks = jax.random.split(key, 24)
    nrm = lambda k, s: jax.random.normal(k, s, jnp.float32)
    D, L = D_MODEL, DEPTH
    return {
        'x': nrm(ks[0], (BATCH, SEQ, D)),
        'c': nrm(ks[1], (BATCH, D)),
        'ctx': nrm(ks[2], (BATCH, CTX_LEN, D)),
        'c_ctx': nrm(ks[3], (D,)),
        'w_ada': nrm(ks[4], (L, D, N_MOD * D)) * (0.5 * D ** -0.5),
        'b_ada': nrm(ks[5], (L, N_MOD * D)) * 0.02,
        'norm1_g': 1.0 + 0.02 * nrm(ks[6], (L, D)),
        'norm2_g': 1.0 + 0.02 * nrm(ks[7], (L, D)),
        'w_in': nrm(ks[8], (L, D, IN_WIDTH)) * D ** -0.5,
        'w_out': nrm(ks[9], (L, MIX_WIDTH, D)) * MIX_WIDTH ** -0.5,
        'lambda_q1': 0.1 * nrm(ks[10], (L, ATT_HEAD_DIM)),
        'lambda_k1': 0.1 * nrm(ks[11], (L, ATT_HEAD_DIM)),
        'lambda_q2': 0.1 * nrm(ks[12], (L, ATT_HEAD_DIM)),
        'lambda_k2': 0.1 * nrm(ks[13], (L, ATT_HEAD_DIM)),
        'subln_g': 1.0 + 0.02 * nrm(ks[14], (L, ATT_V_DIM)),
        'pool_w': nrm(ks[15], (L, POOL_GROUPS, POOL_GROUP_DIM, POOL_GROUP_DIM)) * POOL_GROUP_DIM ** -0.5,
        'pool_scale': 1.0 + 0.1 * nrm(ks[16], (L, POOL_WIDTH)),
        'conv_w': nrm(ks[17], (L, CONV_K, CONV_WIDTH)) * CONV_K ** -0.5,
        'w_router': nrm(ks[18], (L, D, N_EXPERTS)) * D ** -0.5,
        'w_gate': nrm(ks[19], (L, N_EXPERTS, D, EXPERT_FF)) * D ** -0.5,
        'w_up': nrm(ks[20], (L, N_EXPERTS, D, EXPERT_FF)) * D ** -0.5,
        'w_down': nrm(ks[21], (L, N_EXPERTS, EXPERT_FF, D)) * EXPERT_FF ** -0.5,
        'final_g': 1.0 + 0.02 * nrm(ks[22], (D,)),
    }


def reference(x, c, ctx, c_ctx, w_ada, b_ada, norm1_g, norm2_g, w_in, w_out, lambda_q1, lambda_k1,
              lambda_q2, lambda_k2, subln_g, pool_w, pool_scale, conv_w, w_router, w_gate, w_up, w_down,
              final_g):
    B, n, D = x.shape
    rows = n // GRID_W
    row = jnp.repeat(jnp.arange(rows), GRID_W)
    col = jnp.tile(jnp.arange(GRID_W), rows)
    cos, sin = axial_rope_tables(row, col, x.dtype)
    sc = jax.nn.silu(c)
    scc = jax.nn.silu(c_ctx)
    xc = ctx
    for l in range(DEPTH):
        last = l == DEPTH - 1
        mod_x = (sc @ w_ada[l] + b_ada[l]).reshape(B, N_MOD, 1, D)
        mod_c = (scc @ w_ada[l] + b_ada[l]).reshape(N_MOD, D)
        lam_init = 0.8 - 0.6 * math.exp(-0.3 * l)
        lam = (jnp.exp(jnp.sum(lambda_q1[l] * lambda_k1[l]).astype(jnp.float32))
               - jnp.exp(jnp.sum(lambda_q2[l] * lambda_k2[l]).astype(jnp.float32)) + lam_init)
        hx = modulate(rmsnorm(x, norm1_g[l]), mod_x[:, 0], mod_x[:, 1])
        hc = modulate(rmsnorm(xc, norm1_g[l]), mod_c[0], mod_c[1])
        qx, kx, vx, px, bx, cx, ix = split_groups(hx @ w_in[l])
        qc, kc, vc, pc, bc, ccg, ic = split_groups(hc @ w_in[l])
        qx = apply_rope(qx, cos, sin)
        kx = apply_rope(kx, cos, sin)
        k_all = jnp.concatenate([kc, kx], axis=1)
        v_all = jnp.concatenate([vc, vx], axis=1)
        att_x = diff_attn_blocked(qx, k_all, v_all, lam)
        mix_x = mixer_output(att_x, px, bx, cx, ix, subln_g[l], lam_init, pool_w[l], pool_scale[l],
                             conv_w[l], w_out[l])
        x = x + mod_x[:, 2] * mix_x
        h2 = modulate(rmsnorm(x, norm2_g[l]), mod_x[:, 3], mod_x[:, 4])
        x = x + mod_x[:, 5] * expert_choice_ffn(h2, w_router[l], w_gate[l], w_up[l], w_down[l])
        if not last:
            att_c = diff_attn_core(qc, kc, vc, lam)
            mix_c = mixer_output(att_c, pc, bc, ccg, ic, subln_g[l], lam_init, pool_w[l], pool_scale[l],
                                 conv_w[l], w_out[l])
            xc = xc + mod_c[2] * mix_c
            h2c = modulate(rmsnorm(xc, norm2_g[l]), mod_c[3], mod_c[4])
            xc = xc + mod_c[5] * expert_choice_ffn(h2c, w_router[l], w_gate[l], w_up[l], w_down[l])
    return rmsnorm(x, final_g)
```

```python
import functools
import math

import jax
import jax.numpy as jnp
from jax import lax
from jax.experimental import pallas as pl
from jax.experimental.pallas import tpu as pltpu

F32 = jnp.float32
BF16 = jnp.bfloat16

D_MODEL = 2048
N_MOD = 6
GRID_W = 64
ATT_HEADS = 8
ATT_HEAD_DIM = 64
ATT_V_DIM = 2 * ATT_HEAD_DIM
QK_WIDTH = ATT_HEADS * 2 * ATT_HEAD_DIM
ATT_WIDTH = ATT_HEADS * ATT_V_DIM
ROPE_BASE = 10000.0
AXIS_ROT = ATT_HEAD_DIM // 2
POOL_WINDOWS = (2, 4, 8, 16)
POOL_WIDTH = D_MODEL // 4
POOL_GROUP_DIM = POOL_WIDTH // len(POOL_WINDOWS)
CONV_WIDTH = D_MODEL // 4
MIX_WIDTH = ATT_WIDTH + POOL_WIDTH + CONV_WIDTH
REST_WIDTH = POOL_WIDTH + 3 * CONV_WIDTH
IN_WIDTH = 2 * QK_WIDTH + ATT_WIDTH + REST_WIDTH
N_EXPERTS = 16
EXPERT_FF = 1024
CAP_FACTOR = 2
EPS = 1e-6

LANES = 128
SUBLANES = 8
HALO = SUBLANES
TM = 512
TQ = 256
N_CHUNK = 512
FF_TILE = 256
ROUTER_LANES = LANES
VMEM_LIMIT = 56 * 1024 * 1024


def _cparams(n_grid_dims):
    return pltpu.CompilerParams(
        dimension_semantics=("arbitrary",) * n_grid_dims,
        vmem_limit_bytes=VMEM_LIMIT,
    )


def _resident(block_shape, index_map):
    return pl.BlockSpec(block_shape, index_map, pipeline_mode=pl.Buffered(1))


def _inproj_body(x_ref, g_ref, shift_ref, scale_ref, cos_ref, sin_ref, w_ref, qkv_ref, r_ref):
    x = x_ref[0]
    ms = jnp.mean(x * x, axis=-1, keepdims=True)
    h = (x * lax.rsqrt(ms + EPS)) * g_ref[...]
    h = h * (1.0 + scale_ref[0]) + shift_ref[0]
    hb = h.astype(BF16)
    cos = cos_ref[...]
    sin = sin_ref[...]
    lane = lax.broadcasted_iota(jnp.int32, cos.shape, 1)
    first_half = (lane % ATT_HEAD_DIM) < AXIS_ROT
    n_qk = 2 * QK_WIDTH // N_CHUNK
    n_qkv = n_qk + ATT_WIDTH // N_CHUNK
    for c in range(IN_WIDTH // N_CHUNK):
        u = jnp.dot(hb, w_ref[:, c * N_CHUNK:(c + 1) * N_CHUNK], preferred_element_type=F32)
        if c < n_qk:
            for s in range(N_CHUNK // LANES):
                us = u[:, s * LANES:(s + 1) * LANES]
                partner = jnp.where(first_half, pltpu.roll(us, LANES - AXIS_ROT, 1), pltpu.roll(us, AXIS_ROT, 1))
                o = us * cos + partner * sin
                if c < n_qk // 2:
                    o = o * (ATT_HEAD_DIM ** -0.5)
                col = c * N_CHUNK + s * LANES
                qkv_ref[0, :, col:col + LANES] = o.astype(BF16)
        elif c < n_qkv:
            qkv_ref[0, :, c * N_CHUNK:(c + 1) * N_CHUNK] = u.astype(BF16)
        else:
            col = (c - n_qkv) * N_CHUNK
            r_ref[0, :, col:col + N_CHUNK] = u


def _inproj(x, g, shift, scale, cos, sin, w, n_lat):
    B, S, D = x.shape
    n_lat_tiles = n_lat // TM
    mod_idx = lambda b, i: (jnp.where(i < n_lat_tiles, b, B), 0, 0)
    return pl.pallas_call(
        _inproj_body,
        grid=(B, S // TM),
        in_specs=[
            pl.BlockSpec((1, TM, D), lambda b, i: (b, i, 0)),
            _resident((1, D), lambda b, i: (0, 0)),
            pl.BlockSpec((1, 1, D), mod_idx),
            pl.BlockSpec((1, 1, D), mod_idx),
            pl.BlockSpec((TM, LANES), lambda b, i: (i, 0)),
            pl.BlockSpec((TM, LANES), lambda b, i: (i, 0)),
            _resident((D, IN_WIDTH), lambda b, i: (0, 0)),
        ],
        out_specs=[
            pl.BlockSpec((1, TM, 2 * QK_WIDTH + ATT_WIDTH), lambda b, i: (b, i, 0)),
            pl.BlockSpec((1, TM, REST_WIDTH), lambda b, i: (b, i, 0)),
        ],
        out_shape=[
            jax.ShapeDtypeStruct((B, S, 2 * QK_WIDTH + ATT_WIDTH), BF16),
            jax.ShapeDtypeStruct((B, S, REST_WIDTH), F32),
        ],
        compiler_params=_cparams(2),
        name="inproj",
    )(x, g, shift, scale, cos, sin, w)


def _attn_body(lq1_ref, lk1_ref, lq2_ref, lk2_ref, sg_ref, q_ref, k_ref, v_ref, o_ref,
               *, n_lat, n_ctx, tk, lam_init):
    qi = pl.program_id(2)
    n_lat_tiles = n_lat // TQ
    n_ctx_tiles = n_ctx // TQ

    def attend(key_start, key_chunk, n_chunks):
        q = q_ref[0]
        lane = lax.broadcasted_iota(jnp.int32, q.shape, 1)
        zero = jnp.zeros_like(q)
        qs = jnp.concatenate([jnp.where(lane < ATT_HEAD_DIM, q, zero),
                              jnp.where(lane >= ATT_HEAD_DIM, q, zero)], axis=0)
        ones = jnp.ones((key_chunk, LANES), BF16)

        def step(j, carry):
            m, acc = carry
            start = pl.multiple_of(key_start + j * key_chunk, LANES)
            kc = k_ref[0, pl.ds(start, key_chunk), :]
            vc = v_ref[0, pl.ds(start, key_chunk), :]
            s = lax.dot_general(qs, kc, (((1,), (1,)), ((), ())), preferred_element_type=F32)
            m_new = jnp.maximum(m, jnp.max(s, axis=1, keepdims=True))
            alpha = jnp.exp(m - m_new)
            p = jnp.exp(s - m_new).astype(BF16)
            vext = jnp.concatenate([vc, ones], axis=1)
            acc = alpha * acc + jnp.dot(p, vext, preferred_element_type=F32)
            return m_new, acc

        m0 = jnp.full((2 * TQ, 1), -jnp.inf, F32)
        acc0 = jnp.zeros((2 * TQ, 2 * LANES), F32)
        _, acc = lax.fori_loop(0, n_chunks, step, (m0, acc0))
        a0 = acc[:TQ, :LANES] / acc[:TQ, LANES:LANES + 1]
        a1 = acc[TQ:, :LANES] / acc[TQ:, LANES:LANES + 1]
        lam = (jnp.exp(jnp.sum(lq1_ref[...] * lk1_ref[...], axis=1, keepdims=True))
               - jnp.exp(jnp.sum(lq2_ref[...] * lk2_ref[...], axis=1, keepdims=True)) + lam_init)
        o = a0 - lam * a1
        y = o * lax.rsqrt(jnp.mean(o * o, axis=-1, keepdims=True) + EPS)
        o_ref[0] = ((y * sg_ref[...]) * (1.0 - lam_init)).astype(o_ref.dtype)

    @pl.when(qi < n_lat_tiles)
    def _():
        attend(0, tk, (n_lat + n_ctx) // tk)

    @pl.when((qi >= n_lat_tiles) & (qi < n_lat_tiles + n_ctx_tiles))
    def _():
        attend(n_lat, TQ, n_ctx // TQ)

    @pl.when(qi >= n_lat_tiles + n_ctx_tiles)
    def _():
        o_ref[...] = jnp.zeros_like(o_ref)


def _attention(qkv, lq1, lk1, lq2, lk2, sg, n_lat, n_ctx, lam_init):
    B, S, _ = qkv.shape
    n_keys = n_lat + n_ctx
    tk = 768 if n_keys % 768 == 0 else TQ
    small = lambda b, h, i: (0, 0)
    return pl.pallas_call(
        functools.partial(_attn_body, n_lat=n_lat, n_ctx=n_ctx, tk=tk, lam_init=lam_init),
        grid=(B, ATT_HEADS, S // TQ),
        in_specs=[
            pl.BlockSpec((1, ATT_HEAD_DIM), small),
            pl.BlockSpec((1, ATT_HEAD_DIM), small),
            pl.BlockSpec((1, ATT_HEAD_DIM), small),
            pl.BlockSpec((1, ATT_HEAD_DIM), small),
            pl.BlockSpec((1, ATT_V_DIM), small),
            pl.BlockSpec((1, TQ, LANES), lambda b, h, i: (b, i, h)),
            pl.BlockSpec((1, S, LANES), lambda b, h, i: (b, 0, ATT_HEADS + h)),
            pl.BlockSpec((1, S, LANES), lambda b, h, i: (b, 0, 2 * ATT_HEADS + h)),
        ],
        out_specs=pl.BlockSpec((1, TQ, LANES), lambda b, h, i: (b, i, h)),
        out_shape=jax.ShapeDtypeStruct((B, S, ATT_WIDTH), BF16),
        compiler_params=_cparams(3),
        name="diff_attn",
    )(lq1, lk1, lq2, lk2, sg, qkv, qkv, qkv)


def _mix_body(att_ref, r_ref, rprev_ref, rnext_ref, x_ref, wout_ref, poolw_ref, pscale_ref, convw_ref,
              gate_ref, shift_ref, scale_ref, g2_ref, wr_ref,
              xo_ref, h2_ref, aff_ref, pext_ref, zext_ref, *, n_lat, n_ctx):
    i = pl.program_id(1)
    is_ctx = i >= n_lat // TM
    seq_lo = jnp.where(is_ctx, n_lat, 0)
    seq_hi = jnp.where(is_ctx, n_lat + n_ctx, n_lat)
    row0 = i * TM

    def in_seq(first_row, n_rows):
        rows = first_row + lax.broadcasted_iota(jnp.int32, (n_rows, 1), 0)
        return (rows >= seq_lo) & (rows < seq_hi)

    P0, GB0, GC0, XI0 = 0, POOL_WIDTH, POOL_WIDTH + CONV_WIDTH, POOL_WIDTH + 2 * CONV_WIDTH

    def fill(dst_row, src_ref, first_row, n_rows):
        ok = in_seq(first_row, n_rows)
        src = src_ref[0]
        pext_ref[dst_row:dst_row + n_rows, :] = jnp.where(ok, src[:, P0:P0 + POOL_WIDTH], 0.0)
        z = src[:, GC0:GC0 + CONV_WIDTH] * src[:, XI0:XI0 + CONV_WIDTH]
        zext_ref[dst_row:dst_row + n_rows, :] = jnp.where(ok, z, 0.0)

    fill(0, rprev_ref, row0 - HALO, HALO)
    fill(HALO, r_ref, row0, TM)
    fill(HALO + TM, rnext_ref, row0 + TM, HALO)

    t = row0 + lax.broadcasted_iota(jnp.int32, (TM, 1), 0)
    wout_row = ATT_WIDTH
    mix = jnp.dot(att_ref[0], wout_ref[0:ATT_WIDTH, :], preferred_element_type=F32)
    for g, w in enumerate(POOL_WINDOWS):
        c0 = g * POOL_GROUP_DIM
        back, fwd = w // 2, w - w // 2
        tot = pext_ref[HALO - back:HALO - back + TM, c0:c0 + POOL_GROUP_DIM]
        for j in range(-back + 1, fwd):
            tot = tot + pext_ref[HALO + j:HALO + j + TM, c0:c0 + POOL_GROUP_DIM]
        cnt = jnp.minimum(t + fwd, seq_hi) - jnp.maximum(t - back, seq_lo)
        cnt = jnp.maximum(cnt, 1).astype(F32)
        f = tot / cnt - pext_ref[HALO:HALO + TM, c0:c0 + POOL_GROUP_DIM]
        yg = jnp.dot(f.astype(BF16), poolw_ref[g].astype(BF16), preferred_element_type=F32)
        yg = yg * pscale_ref[:, c0:c0 + POOL_GROUP_DIM]
        mix = mix + jnp.dot(yg.astype(BF16), wout_ref[wout_row + c0:wout_row + c0 + POOL_GROUP_DIM, :],
                            preferred_element_type=F32)
    wout_row += POOL_WIDTH
    cw = convw_ref[...]
    conv = (zext_ref[HALO - 1:HALO - 1 + TM, :] * cw[0:1, :]
            + zext_ref[HALO:HALO + TM, :] * cw[1:2, :]
            + zext_ref[HALO + 1:HALO + 1 + TM, :] * cw[2:3, :])
    conv = r_ref[0, :, GB0:GB0 + CONV_WIDTH] * conv
    mix = mix + jnp.dot(conv.astype(BF16), wout_ref[wout_row:wout_row + CONV_WIDTH, :],
                        preferred_element_type=F32)

    xn = x_ref[0] + gate_ref[0] * mix
    xo_ref[0] = xn
    ms = jnp.mean(xn * xn, axis=-1, keepdims=True)
    h2 = (xn * lax.rsqrt(ms + EPS)) * g2_ref[...]
    h2 = h2 * (1.0 + scale_ref[0]) + shift_ref[0]
    h2_ref[0] = h2
    logits = jnp.dot(h2, wr_ref[...], preferred_element_type=F32, precision=lax.Precision.HIGHEST)
    lane = lax.broadcasted_iota(jnp.int32, logits.shape, 1)
    logits = jnp.where(lane < N_EXPERTS, logits, -jnp.inf)
    e = jnp.exp(logits - jnp.max(logits, axis=-1, keepdims=True))
    aff_ref[0] = e / jnp.sum(e, axis=-1, keepdims=True)


def _mixer(att, r, x, wout, poolw, pscale, convw, gate, shift, scale, g2, wr, n_lat, n_ctx):
    B, S, D = x.shape
    n_lat_tiles = n_lat // TM
    halo_blocks = TM // HALO
    last_halo = S // HALO - 1
    row = lambda b, i: (b, i, 0)
    mod_idx = lambda b, i: (jnp.where(i < n_lat_tiles, b, B), 0, 0)
    const2 = lambda b, i: (0, 0)
    return pl.pallas_call(
        functools.partial(_mix_body, n_lat=n_lat, n_ctx=n_ctx),
        grid=(B, S // TM),
        in_specs=[
            pl.BlockSpec((1, TM, ATT_WIDTH), row),
            pl.BlockSpec((1, TM, REST_WIDTH), row),
            pl.BlockSpec((1, HALO, REST_WIDTH), lambda b, i: (b, jnp.maximum(i * halo_blocks - 1, 0), 0)),
            pl.BlockSpec((1, HALO, REST_WIDTH), lambda b, i: (b, jnp.minimum((i + 1) * halo_blocks, last_halo), 0)),
            pl.BlockSpec((1, TM, D), row),
            _resident((MIX_WIDTH, D), const2),
            _resident((len(POOL_WINDOWS), POOL_GROUP_DIM, POOL_GROUP_DIM), lambda b, i: (0, 0, 0)),
            _resident((1, POOL_WIDTH), const2),
            _resident((3, CONV_WIDTH), const2),
            pl.BlockSpec((1, 1, D), mod_idx),
            pl.BlockSpec((1, 1, D), mod_idx),
            pl.BlockSpec((1, 1, D), mod_idx),
            _resident((1, D), const2),
            _resident((D, ROUTER_LANES), const2),
        ],
        out_specs=[
            pl.BlockSpec((1, TM, D), row),
            pl.BlockSpec((1, TM, D), row),
            pl.BlockSpec((1, TM, ROUTER_LANES), row),
        ],
        out_shape=[
            jax.ShapeDtypeStruct((B, S, D), F32),
            jax.ShapeDtypeStruct((B, S, D), F32),
            jax.ShapeDtypeStruct((B, S, ROUTER_LANES), F32),
        ],
        scratch_shapes=[
            pltpu.VMEM((TM + 2 * HALO, POOL_WIDTH), F32),
            pltpu.VMEM((TM + 2 * HALO, CONV_WIDTH), F32),
        ],
        compiler_params=_cparams(2),
        name="mixer_out",
    )(att, r, r, r, x, wout, poolw, pscale, convw, gate, shift, scale, g2, wr)


def _ffn_body(xg_ref, wg_ref, wu_ref, wd_ref, y_ref):
    f = pl.program_id(2)
    xg = xg_ref[0]
    a = jnp.dot(xg, wg_ref[0].astype(BF16), preferred_element_type=F32)
    u = jnp.dot(xg, wu_ref[0].astype(BF16), preferred_element_type=F32)
    hm = (a * jax.nn.sigmoid(a) * u).astype(BF16)
    y = jnp.dot(hm, wd_ref[0].astype(BF16), preferred_element_type=F32)

    @pl.when(f == 0)
    def _():
        y_ref[0] = y

    @pl.when(f > 0)
    def _():
        y_ref[0] += y


def _expert_ffn(xg, w_gate, w_up, w_down, layer, n_tok_blocks):
    E, T, D = xg.shape
    tb = T // n_tok_blocks
    return pl.pallas_call(
        _ffn_body,
        grid=(E, n_tok_blocks, EXPERT_FF // FF_TILE),
        in_specs=[
            pl.BlockSpec((1, tb, D), lambda e, t, f: (e, t, 0)),
            pl.BlockSpec((None, 1, D, FF_TILE), lambda e, t, f: (layer, e, 0, f)),
            pl.BlockSpec((None, 1, D, FF_TILE), lambda e, t, f: (layer, e, 0, f)),
            pl.BlockSpec((None, 1, FF_TILE, D), lambda e, t, f: (layer, e, f, 0)),
        ],
        out_specs=pl.BlockSpec((1, tb, D), lambda e, t, f: (e, t, 0)),
        out_shape=jax.ShapeDtypeStruct((E, T, D), F32),
        compiler_params=_cparams(3),
        name="expert_ffn",
    )(xg, w_gate, w_up, w_down)


def _final_body(x_ref, g_ref, o_ref):
    x = x_ref[0]
    o_ref[0] = (x * lax.rsqrt(jnp.mean(x * x, axis=-1, keepdims=True) + EPS)) * g_ref[...]


def _final_norm(x, g, n_lat):
    B, _, D = x.shape
    return pl.pallas_call(
        _final_body,
        grid=(B, n_lat // TM),
        in_specs=[pl.BlockSpec((1, TM, D), lambda b, i: (b, i, 0)),
                  pl.BlockSpec((1, D), lambda b, i: (0, 0))],
        out_specs=pl.BlockSpec((1, TM, D), lambda b, i: (b, i, 0)),
        out_shape=jax.ShapeDtypeStruct((B, n_lat, D), F32),
        compiler_params=_cparams(2),
        name="final_norm",
    )(x, g)


def _rope_tables(n_lat, s_tot):
    inv = 1.0 / (ROPE_BASE ** (jnp.arange(0, AXIS_ROT, 2, dtype=F32) / AXIS_ROT))
    pos = jnp.arange(n_lat)
    ang = jnp.concatenate([(pos // GRID_W)[:, None].astype(F32) * inv,
                           (pos % GRID_W)[:, None].astype(F32) * inv], axis=-1)
    cos, sin = jnp.cos(ang), jnp.sin(ang)
    cos_l = jnp.tile(cos, (1, LANES // AXIS_ROT))
    sin_l = jnp.tile(jnp.concatenate([-sin, sin], axis=-1), (1, LANES // ATT_HEAD_DIM))
    pad = s_tot - n_lat
    cos_l = jnp.concatenate([cos_l, jnp.ones((pad, LANES), F32)], axis=0)
    sin_l = jnp.concatenate([sin_l, jnp.zeros((pad, LANES), F32)], axis=0)
    return cos_l, sin_l


def kernel(x, c, ctx, c_ctx, w_ada, b_ada, norm1_g, norm2_g, w_in, w_out, lambda_q1, lambda_k1, lambda_q2,
           lambda_k2, subln_g, pool_w, pool_scale, conv_w, w_router, w_gate, w_up, w_down, final_g):
    B, n, D = x.shape
    n_ctx = ctx.shape[1]
    depth = w_in.shape[0]
    assert D == D_MODEL and n % TM == 0 and n_ctx % TQ == 0 and n % GRID_W == 0
    ctx_pad = -(-n_ctx // TM) * TM
    S = n + ctx_pad
    X = jnp.concatenate([x, ctx, jnp.zeros((B, ctx_pad - n_ctx, D), x.dtype)], axis=1)
    cos_l, sin_l = _rope_tables(n, S)
    sc_all = jax.nn.silu(jnp.concatenate([c, c_ctx[None]], axis=0))
    cap, cap_ctx = CAP_FACTOR * n // N_EXPERTS, CAP_FACTOR * n_ctx // N_EXPERTS
    wr_pad = jnp.pad(w_router, ((0, 0), (0, 0), (0, ROUTER_LANES - N_EXPERTS)))

    for l in range(depth):
        lam_init = 0.8 - 0.6 * math.exp(-0.3 * l)
        mod = (jnp.dot(sc_all, w_ada[l], precision=lax.Precision.HIGHEST) + b_ada[l]).reshape(B + 1, N_MOD, 1, D)
        qkv, r = _inproj(X, norm1_g[l][None], mod[:, 0], mod[:, 1], cos_l, sin_l, w_in[l].astype(BF16), n)
        att = _attention(qkv, lambda_q1[l][None], lambda_k1[l][None], lambda_q2[l][None], lambda_k2[l][None],
                         subln_g[l][None], n, n_ctx, lam_init)
        X, h2, aff = _mixer(att, r, X, w_out[l].astype(BF16), pool_w[l], pool_scale[l][None], conv_w[l],
                            mod[:, 2], mod[:, 3], mod[:, 4], norm2_g[l][None], wr_pad[l], n, n_ctx)
        aff = aff[..., :N_EXPERTS]
        g_lat, i_lat = lax.top_k(aff[:, :n].transpose(0, 2, 1), cap)
        g_ctx, i_ctx = lax.top_k(aff[:, n:n + n_ctx].transpose(0, 2, 1), cap_ctx)
        idx = jnp.concatenate([i_lat, i_ctx + n], axis=2)
        gsel = jnp.concatenate([g_lat, g_ctx], axis=2)
        C = cap + cap_ctx
        xg = jnp.take_along_axis(h2, idx.reshape(B, N_EXPERTS * C)[..., None], axis=1)
        xg = xg.reshape(B, N_EXPERTS, C, D).transpose(1, 0, 2, 3).reshape(N_EXPERTS, B * C, D).astype(BF16)
        y = _expert_ffn(xg, w_gate, w_up, w_down, l, B)
        y = y.reshape(N_EXPERTS, B, C, D).transpose(1, 0, 2, 3) * gsel[..., None]
        moe = jax.vmap(lambda ib, yb: jnp.zeros((S, D), F32).at[ib.reshape(-1)].add(yb.reshape(-1, D)))(idx, y)
        gate5 = jnp.concatenate([jnp.broadcast_to(mod[:B, 5], (B, n, D)),
                                 jnp.broadcast_to(mod[B:, 5], (B, ctx_pad, D))], axis=1)
        X = X + gate5 * moe
    return _final_norm(X, final_g[None], n)
```

```python
import functools
import math

import jax
import jax.numpy as jnp
from jax import lax
from jax.experimental import pallas as pl
from jax.experimental.pallas import tpu as pltpu

F32 = jnp.float32
BF16 = jnp.bfloat16

D_MODEL = 2048
N_MOD = 6
GRID_W = 64
ATT_HEADS = 8
ATT_HEAD_DIM = 64
ATT_V_DIM = 2 * ATT_HEAD_DIM
QK_WIDTH = ATT_HEADS * 2 * ATT_HEAD_DIM
ATT_WIDTH = ATT_HEADS * ATT_V_DIM
ROPE_BASE = 10000.0
AXIS_ROT = ATT_HEAD_DIM // 2
POOL_WINDOWS = (2, 4, 8, 16)
POOL_WIDTH = D_MODEL // 4
POOL_GROUP_DIM = POOL_WIDTH // len(POOL_WINDOWS)
CONV_WIDTH = D_MODEL // 4
MIX_WIDTH = ATT_WIDTH + POOL_WIDTH + CONV_WIDTH
REST_WIDTH = POOL_WIDTH + 3 * CONV_WIDTH
IN_WIDTH = 2 * QK_WIDTH + ATT_WIDTH + REST_WIDTH
N_EXPERTS = 16
EXPERT_FF = 1024
CAP_FACTOR = 2
EPS = 1e-6
QK_SCALE = ATT_HEAD_DIM ** -0.5 * math.log2(math.e)

LANES = 128
SUBLANES = 8
HALO = SUBLANES
TM = 512
TQ = 512
N_CHUNK = 512
FF_TILE = 256
ROUTER_LANES = LANES
VMEM_LIMIT = 56 * 1024 * 1024


def _cparams(n_grid_dims):
    return pltpu.CompilerParams(
        dimension_semantics=("arbitrary",) * n_grid_dims,
        vmem_limit_bytes=VMEM_LIMIT,
    )


def _resident(block_shape, index_map):
    return pl.BlockSpec(block_shape, index_map, pipeline_mode=pl.Buffered(1))


def _inproj_body(x_ref, g_ref, shift_ref, scale_ref, cos_ref, sin_ref, w_ref, qkv_ref, r_ref):
    x = x_ref[0]
    ms = jnp.mean(x * x, axis=-1, keepdims=True)
    h = (x * lax.rsqrt(ms + EPS)) * g_ref[...]
    h = h * (1.0 + scale_ref[0]) + shift_ref[0]
    hb = h.astype(BF16)
    cos = cos_ref[...]
    sin = sin_ref[...]
    lane = lax.broadcasted_iota(jnp.int32, cos.shape, 1)
    first_half = (lane % ATT_HEAD_DIM) < AXIS_ROT
    n_qk = 2 * QK_WIDTH // N_CHUNK
    n_qkv = n_qk + ATT_WIDTH // N_CHUNK
    for c in range(IN_WIDTH // N_CHUNK):
        u = jnp.dot(hb, w_ref[:, c * N_CHUNK:(c + 1) * N_CHUNK], preferred_element_type=F32)
        if c < n_qk:
            for s in range(N_CHUNK // LANES):
                us = u[:, s * LANES:(s + 1) * LANES]
                partner = jnp.where(first_half, pltpu.roll(us, LANES - AXIS_ROT, 1), pltpu.roll(us, AXIS_ROT, 1))
                o = us * cos + partner * sin
                if c < n_qk // 2:
                    o = o * QK_SCALE
                col = c * N_CHUNK + s * LANES
                qkv_ref[0, :, col:col + LANES] = o.astype(BF16)
        elif c < n_qkv:
            qkv_ref[0, :, c * N_CHUNK:(c + 1) * N_CHUNK] = u.astype(BF16)
        else:
            col = (c - n_qkv) * N_CHUNK
            r_ref[0, :, col:col + N_CHUNK] = u


def _inproj(x, g, shift, scale, cos, sin, w, n_lat):
    B, S, D = x.shape
    n_lat_tiles = n_lat // TM
    mod_idx = lambda b, i: (jnp.where(i < n_lat_tiles, b, B), 0, 0)
    return pl.pallas_call(
        _inproj_body,
        grid=(B, S // TM),
        in_specs=[
            pl.BlockSpec((1, TM, D), lambda b, i: (b, i, 0)),
            _resident((1, D), lambda b, i: (0, 0)),
            pl.BlockSpec((1, 1, D), mod_idx),
            pl.BlockSpec((1, 1, D), mod_idx),
            pl.BlockSpec((TM, LANES), lambda b, i: (i, 0)),
            pl.BlockSpec((TM, LANES), lambda b, i: (i, 0)),
            _resident((D, IN_WIDTH), lambda b, i: (0, 0)),
        ],
        out_specs=[
            pl.BlockSpec((1, TM, 2 * QK_WIDTH + ATT_WIDTH), lambda b, i: (b, i, 0)),
            pl.BlockSpec((1, TM, REST_WIDTH), lambda b, i: (b, i, 0)),
        ],
        out_shape=[
            jax.ShapeDtypeStruct((B, S, 2 * QK_WIDTH + ATT_WIDTH), BF16),
            jax.ShapeDtypeStruct((B, S, REST_WIDTH), F32),
        ],
        compiler_params=_cparams(2),
        name="inproj",
    )(x, g, shift, scale, cos, sin, w)


def _attn_body(lq1_ref, lk1_ref, lq2_ref, lk2_ref, sg_ref, q_ref, k_ref, v_ref, o_ref,
               qs_ref, acc_ref, s_a, s_b, p_a, p_b, *, n_lat, n_ctx, tk, lam_init):
    qi = pl.program_id(2)

    def attend(key_start, kc_len, n_chunks):
        assert n_chunks % 2 == 1
        q = q_ref[0]
        lane = lax.broadcasted_iota(jnp.int32, q.shape, 1)
        zero = jnp.zeros_like(q)
        qs_ref[:TQ, :] = jnp.where(lane < ATT_HEAD_DIM, q, zero)
        qs_ref[TQ:, :] = jnp.where(lane >= ATT_HEAD_DIM, q, zero)
        acc_ref[...] = jnp.zeros_like(acc_ref)
        ones = jnp.ones((kc_len, LANES), BF16)

        def keys_at(j):
            return pl.ds(pl.multiple_of(key_start + j * kc_len, LANES), kc_len)

        def scores(j, s_ref):
            s = lax.dot_general(qs_ref[...], k_ref[0, keys_at(j), :], (((1,), (1,)), ((), ())),
                                preferred_element_type=F32)
            s_ref[:, :kc_len] = s
            return jnp.max(s, axis=1, keepdims=True)

        def softmax(s_ref, p_ref, row_max, m):
            m_new = jnp.maximum(m, row_max)
            p_ref[:, :kc_len] = jnp.exp2(s_ref[:, :kc_len] - m_new).astype(BF16)
            return m_new, jnp.exp2(m - m_new)

        def values(j, p_ref, alpha):
            vext = jnp.concatenate([v_ref[0, keys_at(j), :], ones], axis=1)
            acc_ref[...] = alpha * acc_ref[...] + jnp.dot(p_ref[:, :kc_len], vext, preferred_element_type=F32)

        m = jnp.full((2 * TQ, 1), -jnp.inf, F32)
        mx_a = scores(0, s_a)
        if n_chunks == 1:
            m, al_a = softmax(s_a, p_a, mx_a, m)
            values(0, p_a, al_a)
        else:
            mx_b = scores(1, s_b)
            m, al_a = softmax(s_a, p_a, mx_a, m)

            def pair(i, carry):
                m, mx_b, al_a = carry
                mx_a = scores(2 * i + 2, s_a)
                m, al_b = softmax(s_b, p_b, mx_b, m)
                values(2 * i, p_a, al_a)
                mx_b = scores(2 * i + 3, s_b)
                m, al_a = softmax(s_a, p_a, mx_a, m)
                values(2 * i + 1, p_b, al_b)
                return m, mx_b, al_a

            carry = (m, mx_b, al_a)
            for i in range(n_chunks // 2 - 1):
                carry = pair(i, carry)
            m, mx_b, al_a = carry
            last = n_chunks - 1
            mx_a = scores(last, s_a)
            m, al_b = softmax(s_b, p_b, mx_b, m)
            values(last - 2, p_a, al_a)
            m, al_a = softmax(s_a, p_a, mx_a, m)
            values(last - 1, p_b, al_b)
            values(last, p_a, al_a)

        a0 = acc_ref[:TQ, :LANES] / acc_ref[:TQ, LANES:LANES + 1]
        a1 = acc_ref[TQ:, :LANES] / acc_ref[TQ:, LANES:LANES + 1]
        lam = (jnp.exp(jnp.sum(lq1_ref[...] * lk1_ref[...], axis=1, keepdims=True))
               - jnp.exp(jnp.sum(lq2_ref[...] * lk2_ref[...], axis=1, keepdims=True)) + lam_init)
        o = a0 - lam * a1
        y = o * lax.rsqrt(jnp.mean(o * o, axis=-1, keepdims=True) + EPS)
        o_ref[0] = ((y * sg_ref[...]) * (1.0 - lam_init)).astype(o_ref.dtype)

    @pl.when(qi < n_lat // TQ)
    def _():
        attend(0, tk, (n_lat + n_ctx) // tk)

    @pl.when(qi >= n_lat // TQ)
    def _():
        attend(n_lat, n_ctx, 1)


def _attention(qkv, lq1, lk1, lq2, lk2, sg, n_lat, n_ctx, lam_init):
    B, S, _ = qkv.shape
    n_keys = n_lat + n_ctx
    tk = 768 if n_keys % 768 == 0 else 256
    assert n_keys % tk == 0 and (n_keys // tk) % 2 == 1 and n_ctx <= tk
    small = lambda b, h, i: (0, 0)
    return pl.pallas_call(
        functools.partial(_attn_body, n_lat=n_lat, n_ctx=n_ctx, tk=tk, lam_init=lam_init),
        grid=(B, ATT_HEADS, S // TQ),
        in_specs=[
            pl.BlockSpec((1, ATT_HEAD_DIM), small),
            pl.BlockSpec((1, ATT_HEAD_DIM), small),
            pl.BlockSpec((1, ATT_HEAD_DIM), small),
            pl.BlockSpec((1, ATT_HEAD_DIM), small),
            pl.BlockSpec((1, ATT_V_DIM), small),
            pl.BlockSpec((1, TQ, LANES), lambda b, h, i: (b, i, h)),
            pl.BlockSpec((1, S, LANES), lambda b, h, i: (b, 0, ATT_HEADS + h)),
            pl.BlockSpec((1, S, LANES), lambda b, h, i: (b, 0, 2 * ATT_HEADS + h)),
        ],
        out_specs=pl.BlockSpec((1, TQ, LANES), lambda b, h, i: (b, i, h)),
        out_shape=jax.ShapeDtypeStruct((B, S, ATT_WIDTH), BF16),
        scratch_shapes=[
            pltpu.VMEM((2 * TQ, LANES), BF16),
            pltpu.VMEM((2 * TQ, 2 * LANES), F32),
            pltpu.VMEM((2 * TQ, tk), F32),
            pltpu.VMEM((2 * TQ, tk), F32),
            pltpu.VMEM((2 * TQ, tk), BF16),
            pltpu.VMEM((2 * TQ, tk), BF16),
        ],
        compiler_params=_cparams(3),
        name="diff_attn",
    )(lq1, lk1, lq2, lk2, sg, qkv, qkv, qkv)


def _mix_body(att_ref, r_ref, rprev_ref, rnext_ref, x_ref, wout_ref, poolw_ref, pscale_ref, convw_ref,
              gate_ref, shift_ref, scale_ref, g2_ref, wr_ref,
              xo_ref, h2_ref, aff_ref, pext_ref, zext_ref, *, n_lat, n_ctx):
    i = pl.program_id(1)
    is_ctx = i >= n_lat // TM
    seq_lo = jnp.where(is_ctx, n_lat, 0)
    seq_hi = jnp.where(is_ctx, n_lat + n_ctx, n_lat)
    row0 = i * TM

    def in_seq(first_row, n_rows):
        rows = first_row + lax.broadcasted_iota(jnp.int32, (n_rows, 1), 0)
        return (rows >= seq_lo) & (rows < seq_hi)

    P0, GB0, GC0, XI0 = 0, POOL_WIDTH, POOL_WIDTH + CONV_WIDTH, POOL_WIDTH + 2 * CONV_WIDTH

    def fill(dst_row, src_ref, first_row, n_rows):
        ok = in_seq(first_row, n_rows)
        src = src_ref[0]
        pext_ref[dst_row:dst_row + n_rows, :] = jnp.where(ok, src[:, P0:P0 + POOL_WIDTH], 0.0)
        z = src[:, GC0:GC0 + CONV_WIDTH] * src[:, XI0:XI0 + CONV_WIDTH]
        zext_ref[dst_row:dst_row + n_rows, :] = jnp.where(ok, z, 0.0)

    fill(0, rprev_ref, row0 - HALO, HALO)
    fill(HALO, r_ref, row0, TM)
    fill(HALO + TM, rnext_ref, row0 + TM, HALO)

    t = row0 + lax.broadcasted_iota(jnp.int32, (TM, 1), 0)
    wout_row = ATT_WIDTH
    mix = jnp.dot(att_ref[0], wout_ref[0:ATT_WIDTH, :], preferred_element_type=F32)
    for g, w in enumerate(POOL_WINDOWS):
        c0 = g * POOL_GROUP_DIM
        back, fwd = w // 2, w - w // 2
        tot = pext_ref[HALO - back:HALO - back + TM, c0:c0 + POOL_GROUP_DIM]
        for j in range(-back + 1, fwd):
            tot = tot + pext_ref[HALO + j:HALO + j + TM, c0:c0 + POOL_GROUP_DIM]
        cnt = jnp.minimum(t + fwd, seq_hi) - jnp.maximum(t - back, seq_lo)
        cnt = jnp.maximum(cnt, 1).astype(F32)
        f = tot / cnt - pext_ref[HALO:HALO + TM, c0:c0 + POOL_GROUP_DIM]
        yg = jnp.dot(f.astype(BF16), poolw_ref[g].astype(BF16), preferred_element_type=F32)
        yg = yg * pscale_ref[:, c0:c0 + POOL_GROUP_DIM]
        mix = mix + jnp.dot(yg.astype(BF16), wout_ref[wout_row + c0:wout_row + c0 + POOL_GROUP_DIM, :],
                            preferred_element_type=F32)
    wout_row += POOL_WIDTH
    cw = convw_ref[...]
    conv = (zext_ref[HALO - 1:HALO - 1 + TM, :] * cw[0:1, :]
            + zext_ref[HALO:HALO + TM, :] * cw[1:2, :]
            + zext_ref[HALO + 1:HALO + 1 + TM, :] * cw[2:3, :])
    conv = r_ref[0, :, GB0:GB0 + CONV_WIDTH] * conv
    mix = mix + jnp.dot(conv.astype(BF16), wout_ref[wout_row:wout_row + CONV_WIDTH, :],
                        preferred_element_type=F32)

    xn = x_ref[0] + gate_ref[0] * mix
    xo_ref[0] = xn
    ms = jnp.mean(xn * xn, axis=-1, keepdims=True)
    h2 = (xn * lax.rsqrt(ms + EPS)) * g2_ref[...]
    h2 = h2 * (1.0 + scale_ref[0]) + shift_ref[0]
    h2_ref[0] = h2
    logits = jnp.dot(h2, wr_ref[...], preferred_element_type=F32, precision=lax.Precision.HIGHEST)
    lane = lax.broadcasted_iota(jnp.int32, logits.shape, 1)
    logits = jnp.where(lane < N_EXPERTS, logits, -jnp.inf)
    e = jnp.exp(logits - jnp.max(logits, axis=-1, keepdims=True))
    aff_ref[0] = e / jnp.sum(e, axis=-1, keepdims=True)


def _mixer(att, r, x, wout, poolw, pscale, convw, gate, shift, scale, g2, wr, n_lat, n_ctx):
    B, S, D = x.shape
    n_lat_tiles = n_lat // TM
    halo_blocks = TM // HALO
    last_halo = S // HALO - 1
    row = lambda b, i: (b, i, 0)
    mod_idx = lambda b, i: (jnp.where(i < n_lat_tiles, b, B), 0, 0)
    const2 = lambda b, i: (0, 0)
    return pl.pallas_call(
        functools.partial(_mix_body, n_lat=n_lat, n_ctx=n_ctx),
        grid=(B, S // TM),
        in_specs=[
            pl.BlockSpec((1, TM, ATT_WIDTH), row),
            pl.BlockSpec((1, TM, REST_WIDTH), row),
            pl.BlockSpec((1, HALO, REST_WIDTH), lambda b, i: (b, jnp.maximum(i * halo_blocks - 1, 0), 0)),
            pl.BlockSpec((1, HALO, REST_WIDTH), lambda b, i: (b, jnp.minimum((i + 1) * halo_blocks, last_halo), 0)),
            pl.BlockSpec((1, TM, D), row),
            _resident((MIX_WIDTH, D), const2),
            _resident((len(POOL_WINDOWS), POOL_GROUP_DIM, POOL_GROUP_DIM), lambda b, i: (0, 0, 0)),
            _resident((1, POOL_WIDTH), const2),
            _resident((3, CONV_WIDTH), const2),
            pl.BlockSpec((1, 1, D), mod_idx),
            pl.BlockSpec((1, 1, D), mod_idx),
            pl.BlockSpec((1, 1, D), mod_idx),
            _resident((1, D), const2),
            _resident((D, ROUTER_LANES), const2),
        ],
        out_specs=[
            pl.BlockSpec((1, TM, D), row),
            pl.BlockSpec((1, TM, D), row),
            pl.BlockSpec((1, TM, ROUTER_LANES), row),
        ],
        out_shape=[
            jax.ShapeDtypeStruct((B, S, D), F32),
            jax.ShapeDtypeStruct((B, S, D), F32),
            jax.ShapeDtypeStruct((B, S, ROUTER_LANES), F32),
        ],
        scratch_shapes=[
            pltpu.VMEM((TM + 2 * HALO, POOL_WIDTH), F32),
            pltpu.VMEM((TM + 2 * HALO, CONV_WIDTH), F32),
        ],
        compiler_params=_cparams(2),
        name="mixer_out",
    )(att, r, r, r, x, wout, poolw, pscale, convw, gate, shift, scale, g2, wr)


def _ffn_body(xg_ref, wg_ref, wu_ref, wd_ref, y_ref):
    f = pl.program_id(2)
    xg = xg_ref[0]
    a = jnp.dot(xg, wg_ref[0].astype(BF16), preferred_element_type=F32)
    u = jnp.dot(xg, wu_ref[0].astype(BF16), preferred_element_type=F32)
    hm = (a * jax.nn.sigmoid(a) * u).astype(BF16)
    y = jnp.dot(hm, wd_ref[0].astype(BF16), preferred_element_type=F32)

    @pl.when(f == 0)
    def _():
        y_ref[0] = y

    @pl.when(f > 0)
    def _():
        y_ref[0] += y


def _expert_ffn(xg, w_gate, w_up, w_down, layer, n_tok_blocks):
    E, T, D = xg.shape
    tb = T // n_tok_blocks
    return pl.pallas_call(
        _ffn_body,
        grid=(E, n_tok_blocks, EXPERT_FF // FF_TILE),
        in_specs=[
            pl.BlockSpec((1, tb, D), lambda e, t, f: (e, t, 0)),
            pl.BlockSpec((None, 1, D, FF_TILE), lambda e, t, f: (layer, e, 0, f)),
            pl.BlockSpec((None, 1, D, FF_TILE), lambda e, t, f: (layer, e, 0, f)),
            pl.BlockSpec((None, 1, FF_TILE, D), lambda e, t, f: (layer, e, f, 0)),
        ],
        out_specs=pl.BlockSpec((1, tb, D), lambda e, t, f: (e, t, 0)),
        out_shape=jax.ShapeDtypeStruct((E, T, D), F32),
        compiler_params=_cparams(3),
        name="expert_ffn",
    )(xg, w_gate, w_up, w_down)


def _final_body(x_ref, g_ref, o_ref):
    x = x_ref[0]
    o_ref[0] = (x * lax.rsqrt(jnp.mean(x * x, axis=-1, keepdims=True) + EPS)) * g_ref[...]


def _final_norm(x, g, n_lat):
    B, _, D = x.shape
    return pl.pallas_call(
        _final_body,
        grid=(B, n_lat // TM),
        in_specs=[pl.BlockSpec((1, TM, D), lambda b, i: (b, i, 0)),
                  pl.BlockSpec((1, D), lambda b, i: (0, 0))],
        out_specs=pl.BlockSpec((1, TM, D), lambda b, i: (b, i, 0)),
        out_shape=jax.ShapeDtypeStruct((B, n_lat, D), F32),
        compiler_params=_cparams(2),
        name="final_norm",
    )(x, g)


def _rope_tables(n_lat, s_tot):
    inv = 1.0 / (ROPE_BASE ** (jnp.arange(0, AXIS_ROT, 2, dtype=F32) / AXIS_ROT))
    pos = jnp.arange(n_lat)
    ang = jnp.concatenate([(pos // GRID_W)[:, None].astype(F32) * inv,
                           (pos % GRID_W)[:, None].astype(F32) * inv], axis=-1)
    cos, sin = jnp.cos(ang), jnp.sin(ang)
    cos_l = jnp.tile(cos, (1, LANES // AXIS_ROT))
    sin_l = jnp.tile(jnp.concatenate([-sin, sin], axis=-1), (1, LANES // ATT_HEAD_DIM))
    pad = s_tot - n_lat
    cos_l = jnp.concatenate([cos_l, jnp.ones((pad, LANES), F32)], axis=0)
    sin_l = jnp.concatenate([sin_l, jnp.zeros((pad, LANES), F32)], axis=0)
    return cos_l, sin_l


def kernel(x, c, ctx, c_ctx, w_ada, b_ada, norm1_g, norm2_g, w_in, w_out, lambda_q1, lambda_k1, lambda_q2,
           lambda_k2, subln_g, pool_w, pool_scale, conv_w, w_router, w_gate, w_up, w_down, final_g):
    B, n, D = x.shape
    n_ctx = ctx.shape[1]
    depth = w_in.shape[0]
    assert D == D_MODEL and n % TM == 0 and n % GRID_W == 0 and TQ == TM
    ctx_pad = -(-n_ctx // TM) * TM
    S = n + ctx_pad
    X = jnp.concatenate([x, ctx, jnp.zeros((B, ctx_pad - n_ctx, D), x.dtype)], axis=1)
    cos_l, sin_l = _rope_tables(n, S)
    sc_all = jax.nn.silu(jnp.concatenate([c, c_ctx[None]], axis=0))
    cap, cap_ctx = CAP_FACTOR * n // N_EXPERTS, CAP_FACTOR * n_ctx // N_EXPERTS
    wr_pad = jnp.pad(w_router, ((0, 0), (0, 0), (0, ROUTER_LANES - N_EXPERTS)))

    for l in range(depth):
        lam_init = 0.8 - 0.6 * math.exp(-0.3 * l)
        mod = (jnp.dot(sc_all, w_ada[l], precision=lax.Precision.HIGHEST) + b_ada[l]).reshape(B + 1, N_MOD, 1, D)
        qkv, r = _inproj(X, norm1_g[l][None], mod[:, 0], mod[:, 1], cos_l, sin_l, w_in[l].astype(BF16), n)
        att = _attention(qkv, lambda_q1[l][None], lambda_k1[l][None], lambda_q2[l][None], lambda_k2[l][None],
                         subln_g[l][None], n, n_ctx, lam_init)
        X, h2, aff = _mixer(att, r, X, w_out[l].astype(BF16), pool_w[l], pool_scale[l][None], conv_w[l],
                            mod[:, 2], mod[:, 3], mod[:, 4], norm2_g[l][None], wr_pad[l], n, n_ctx)
        aff = aff[..., :N_EXPERTS]
        g_lat, i_lat = lax.top_k(aff[:, :n].transpose(0, 2, 1), cap)
        g_ctx, i_ctx = lax.top_k(aff[:, n:n + n_ctx].transpose(0, 2, 1), cap_ctx)
        idx = jnp.concatenate([i_lat, i_ctx + n], axis=2)
        gsel = jnp.concatenate([g_lat, g_ctx], axis=2)
        C = cap + cap_ctx
        xg = jnp.take_along_axis(h2, idx.reshape(B, N_EXPERTS * C)[..., None], axis=1)
        xg = xg.reshape(B, N_EXPERTS, C, D).transpose(1, 0, 2, 3).reshape(N_EXPERTS, B * C, D).astype(BF16)
        y = _expert_ffn(xg, w_gate, w_up, w_down, l, B)
        y = y.reshape(N_EXPERTS, B, C, D).transpose(1, 0, 2, 3) * gsel[..., None]
        moe = jax.vmap(lambda ib, yb: jnp.zeros((S, D), F32).at[ib.reshape(-1)].add(yb.reshape(-1, D)))(idx, y)
        gate5 = jnp.concatenate([jnp.broadcast_to(mod[:B, 5], (B, n, D)),
                                 jnp.broadcast_to(mod[B:, 5], (B, ctx_pad, D))], axis=1)
        X = X + gate5 * moe
    return _final_norm(X, final_g[None], n)
```

```python
import functools
import math

import jax
import jax.numpy as jnp
from jax import lax
from jax.experimental import pallas as pl
from jax.experimental.pallas import tpu as pltpu

F32 = jnp.float32
BF16 = jnp.bfloat16

D_MODEL = 2048
N_MOD = 6
GRID_W = 64
ATT_HEADS = 8
ATT_HEAD_DIM = 64
ATT_V_DIM = 2 * ATT_HEAD_DIM
QK_WIDTH = ATT_HEADS * 2 * ATT_HEAD_DIM
ATT_WIDTH = ATT_HEADS * ATT_V_DIM
ROPE_BASE = 10000.0
AXIS_ROT = ATT_HEAD_DIM // 2
POOL_WINDOWS = (2, 4, 8, 16)
POOL_WIDTH = D_MODEL // 4
POOL_GROUP_DIM = POOL_WIDTH // len(POOL_WINDOWS)
CONV_WIDTH = D_MODEL // 4
MIX_WIDTH = ATT_WIDTH + POOL_WIDTH + CONV_WIDTH
REST_WIDTH = POOL_WIDTH + 3 * CONV_WIDTH
IN_WIDTH = 2 * QK_WIDTH + ATT_WIDTH + REST_WIDTH
N_EXPERTS = 16
EXPERT_FF = 1024
CAP_FACTOR = 2
EPS = 1e-6
QK_SCALE = ATT_HEAD_DIM ** -0.5 * math.log2(math.e)

LANES = 128
SUBLANES = 8
HALO = SUBLANES
TM = 512
TQ = 512
N_CHUNK = 512
FF_TILE = 256
ROUTER_LANES = LANES
VMEM_LIMIT = 56 * 1024 * 1024


def _cparams(n_grid_dims):
    return pltpu.CompilerParams(
        dimension_semantics=("arbitrary",) * n_grid_dims,
        vmem_limit_bytes=VMEM_LIMIT,
    )


def _resident(block_shape, index_map):
    return pl.BlockSpec(block_shape, index_map, pipeline_mode=pl.Buffered(1))


def _inproj_body(x_ref, g_ref, shift_ref, scale_ref, cos_ref, sin_ref, w_ref, qkv_ref, r_ref):
    x = x_ref[0]
    ms = jnp.mean(x * x, axis=-1, keepdims=True)
    h = (x * lax.rsqrt(ms + EPS)) * g_ref[...]
    h = h * (1.0 + scale_ref[0]) + shift_ref[0]
    hb = h.astype(BF16)
    cos = cos_ref[...]
    sin = sin_ref[...]
    lane = lax.broadcasted_iota(jnp.int32, cos.shape, 1)
    first_half = (lane % ATT_HEAD_DIM) < AXIS_ROT
    n_qk = 2 * QK_WIDTH // N_CHUNK
    n_qkv = n_qk + ATT_WIDTH // N_CHUNK
    for c in range(IN_WIDTH // N_CHUNK):
        u = jnp.dot(hb, w_ref[:, c * N_CHUNK:(c + 1) * N_CHUNK], preferred_element_type=F32)
        if c < n_qk:
            for s in range(N_CHUNK // LANES):
                us = u[:, s * LANES:(s + 1) * LANES]
                partner = jnp.where(first_half, pltpu.roll(us, LANES - AXIS_ROT, 1), pltpu.roll(us, AXIS_ROT, 1))
                o = us * cos + partner * sin
                if c < n_qk // 2:
                    o = o * QK_SCALE
                col = c * N_CHUNK + s * LANES
                qkv_ref[0, :, col:col + LANES] = o.astype(BF16)
        elif c < n_qkv:
            qkv_ref[0, :, c * N_CHUNK:(c + 1) * N_CHUNK] = u.astype(BF16)
        else:
            col = (c - n_qkv) * N_CHUNK
            r_ref[0, :, col:col + N_CHUNK] = u


def _inproj(x, g, shift, scale, cos, sin, w, n_lat):
    B, S, D = x.shape
    n_lat_tiles = n_lat // TM
    mod_idx = lambda b, i: (jnp.where(i < n_lat_tiles, b, B), 0, 0)
    return pl.pallas_call(
        _inproj_body,
        grid=(B, S // TM),
        in_specs=[
            pl.BlockSpec((1, TM, D), lambda b, i: (b, i, 0)),
            _resident((1, D), lambda b, i: (0, 0)),
            pl.BlockSpec((1, 1, D), mod_idx),
            pl.BlockSpec((1, 1, D), mod_idx),
            pl.BlockSpec((TM, LANES), lambda b, i: (i, 0)),
            pl.BlockSpec((TM, LANES), lambda b, i: (i, 0)),
            _resident((D, IN_WIDTH), lambda b, i: (0, 0)),
        ],
        out_specs=[
            pl.BlockSpec((1, TM, 2 * QK_WIDTH + ATT_WIDTH), lambda b, i: (b, i, 0)),
            pl.BlockSpec((1, TM, REST_WIDTH), lambda b, i: (b, i, 0)),
        ],
        out_shape=[
            jax.ShapeDtypeStruct((B, S, 2 * QK_WIDTH + ATT_WIDTH), BF16),
            jax.ShapeDtypeStruct((B, S, REST_WIDTH), F32),
        ],
        compiler_params=_cparams(2),
        name="inproj",
    )(x, g, shift, scale, cos, sin, w)


def _attn_body(lq1_ref, lk1_ref, lq2_ref, lk2_ref, sg_ref, q_ref, k_ref, v_ref, o_ref,
               qs_ref, acc_ref, s_a, s_b, p_a, p_b, *, n_lat, n_ctx, tk, lam_init):
    qi = pl.program_id(2)

    def attend(key_start, kc_len, n_chunks):
        assert n_chunks % 2 == 1
        q = q_ref[0]
        lane = lax.broadcasted_iota(jnp.int32, q.shape, 1)
        zero = jnp.zeros_like(q)
        qs_ref[:TQ, :] = jnp.where(lane < ATT_HEAD_DIM, q, zero)
        qs_ref[TQ:, :] = jnp.where(lane >= ATT_HEAD_DIM, q, zero)
        acc_ref[...] = jnp.zeros_like(acc_ref)
        ones = jnp.ones((kc_len, LANES), BF16)

        def keys_at(j):
            return pl.ds(pl.multiple_of(key_start + j * kc_len, LANES), kc_len)

        def scores(j, s_ref):
            s = lax.dot_general(qs_ref[...], k_ref[0, keys_at(j), :], (((1,), (1,)), ((), ())),
                                preferred_element_type=F32)
            s_ref[:, :kc_len] = s
            return jnp.max(s, axis=1, keepdims=True)

        def softmax(s_ref, p_ref, row_max, m):
            m_new = jnp.maximum(m, row_max)
            p_ref[:, :kc_len] = jnp.exp2(s_ref[:, :kc_len] - m_new).astype(BF16)
            return m_new, jnp.exp2(m - m_new)

        def values(j, p_ref, alpha):
            vext = jnp.concatenate([v_ref[0, keys_at(j), :], ones], axis=1)
            acc_ref[...] = alpha * acc_ref[...] + jnp.dot(p_ref[:, :kc_len], vext, preferred_element_type=F32)

        m = jnp.full((2 * TQ, 1), -jnp.inf, F32)
        mx_a = scores(0, s_a)
        if n_chunks == 1:
            m, al_a = softmax(s_a, p_a, mx_a, m)
            values(0, p_a, al_a)
        else:
            mx_b = scores(1, s_b)
            m, al_a = softmax(s_a, p_a, mx_a, m)

            def pair(i, carry):
                m, mx_b, al_a = carry
                mx_a = scores(2 * i + 2, s_a)
                m, al_b = softmax(s_b, p_b, mx_b, m)
                values(2 * i, p_a, al_a)
                mx_b = scores(2 * i + 3, s_b)
                m, al_a = softmax(s_a, p_a, mx_a, m)
                values(2 * i + 1, p_b, al_b)
                return m, mx_b, al_a

            carry = (m, mx_b, al_a)
            for i in range(n_chunks // 2 - 1):
                carry = pair(i, carry)
            m, mx_b, al_a = carry
            last = n_chunks - 1
            mx_a = scores(last, s_a)
            m, al_b = softmax(s_b, p_b, mx_b, m)
            values(last - 2, p_a, al_a)
            m, al_a = softmax(s_a, p_a, mx_a, m)
            values(last - 1, p_b, al_b)
            values(last, p_a, al_a)

        a0 = acc_ref[:TQ, :LANES] / acc_ref[:TQ, LANES:LANES + 1]
        a1 = acc_ref[TQ:, :LANES] / acc_ref[TQ:, LANES:LANES + 1]
        lam = (jnp.exp(jnp.sum(lq1_ref[...] * lk1_ref[...], axis=1, keepdims=True))
               - jnp.exp(jnp.sum(lq2_ref[...] * lk2_ref[...], axis=1, keepdims=True)) + lam_init)
        o = a0 - lam * a1
        y = o * lax.rsqrt(jnp.mean(o * o, axis=-1, keepdims=True) + EPS)
        o_ref[0] = ((y * sg_ref[...]) * (1.0 - lam_init)).astype(o_ref.dtype)

    @pl.when(qi < n_lat // TQ)
    def _():
        attend(0, tk, (n_lat + n_ctx) // tk)

    @pl.when(qi >= n_lat // TQ)
    def _():
        attend(n_lat, n_ctx, 1)


def _attention(qkv, lq1, lk1, lq2, lk2, sg, n_lat, n_ctx, lam_init):
    B, S, _ = qkv.shape
    n_keys = n_lat + n_ctx
    tk = 768 if n_keys % 768 == 0 else 256
    assert n_keys % tk == 0 and (n_keys // tk) % 2 == 1 and n_ctx <= tk
    small = lambda b, h, i: (0, 0)
    return pl.pallas_call(
        functools.partial(_attn_body, n_lat=n_lat, n_ctx=n_ctx, tk=tk, lam_init=lam_init),
        grid=(B, ATT_HEADS, S // TQ),
        in_specs=[
            pl.BlockSpec((1, ATT_HEAD_DIM), small),
            pl.BlockSpec((1, ATT_HEAD_DIM), small),
            pl.BlockSpec((1, ATT_HEAD_DIM), small),
            pl.BlockSpec((1, ATT_HEAD_DIM), small),
            pl.BlockSpec((1, ATT_V_DIM), small),
            pl.BlockSpec((1, TQ, LANES), lambda b, h, i: (b, i, h)),
            pl.BlockSpec((1, S, LANES), lambda b, h, i: (b, 0, ATT_HEADS + h)),
            pl.BlockSpec((1, S, LANES), lambda b, h, i: (b, 0, 2 * ATT_HEADS + h)),
        ],
        out_specs=pl.BlockSpec((1, TQ, LANES), lambda b, h, i: (b, i, h)),
        out_shape=jax.ShapeDtypeStruct((B, S, ATT_WIDTH), BF16),
        scratch_shapes=[
            pltpu.VMEM((2 * TQ, LANES), BF16),
            pltpu.VMEM((2 * TQ, 2 * LANES), F32),
            pltpu.VMEM((2 * TQ, tk), F32),
            pltpu.VMEM((2 * TQ, tk), F32),
            pltpu.VMEM((2 * TQ, tk), BF16),
            pltpu.VMEM((2 * TQ, tk), BF16),
        ],
        compiler_params=_cparams(3),
        name="diff_attn",
    )(lq1, lk1, lq2, lk2, sg, qkv, qkv, qkv)


def _mix_body(att_ref, r_ref, rprev_ref, rnext_ref, x_ref, wout_ref, poolw_ref, pscale_ref, convw_ref,
              gate_ref, shift_ref, scale_ref, g2_ref, wr_ref,
              xo_ref, h2_ref, aff_ref, pext_ref, zext_ref, *, n_lat, n_ctx):
    i = pl.program_id(1)
    is_ctx = i >= n_lat // TM
    seq_lo = jnp.where(is_ctx, n_lat, 0)
    seq_hi = jnp.where(is_ctx, n_lat + n_ctx, n_lat)
    row0 = i * TM

    def in_seq(first_row, n_rows):
        rows = first_row + lax.broadcasted_iota(jnp.int32, (n_rows, 1), 0)
        return (rows >= seq_lo) & (rows < seq_hi)

    P0, GB0, GC0, XI0 = 0, POOL_WIDTH, POOL_WIDTH + CONV_WIDTH, POOL_WIDTH + 2 * CONV_WIDTH

    def fill(dst_row, src_ref, first_row, n_rows):
        ok = in_seq(first_row, n_rows)
        src = src_ref[0]
        pext_ref[dst_row:dst_row + n_rows, :] = jnp.where(ok, src[:, P0:P0 + POOL_WIDTH], 0.0)
        z = src[:, GC0:GC0 + CONV_WIDTH] * src[:, XI0:XI0 + CONV_WIDTH]
        zext_ref[dst_row:dst_row + n_rows, :] = jnp.where(ok, z, 0.0)

    fill(0, rprev_ref, row0 - HALO, HALO)
    fill(HALO, r_ref, row0, TM)
    fill(HALO + TM, rnext_ref, row0 + TM, HALO)

    t = row0 + lax.broadcasted_iota(jnp.int32, (TM, 1), 0)
    wout_row = ATT_WIDTH
    mix = jnp.dot(att_ref[0], wout_ref[0:ATT_WIDTH, :], preferred_element_type=F32)
    for g, w in enumerate(POOL_WINDOWS):
        c0 = g * POOL_GROUP_DIM
        back, fwd = w // 2, w - w // 2
        tot = pext_ref[HALO - back:HALO - back + TM, c0:c0 + POOL_GROUP_DIM]
        for j in range(-back + 1, fwd):
            tot = tot + pext_ref[HALO + j:HALO + j + TM, c0:c0 + POOL_GROUP_DIM]
        cnt = jnp.minimum(t + fwd, seq_hi) - jnp.maximum(t - back, seq_lo)
        cnt = jnp.maximum(cnt, 1).astype(F32)
        f = tot / cnt - pext_ref[HALO:HALO + TM, c0:c0 + POOL_GROUP_DIM]
        yg = jnp.dot(f.astype(BF16), poolw_ref[g].astype(BF16), preferred_element_type=F32)
        yg = yg * pscale_ref[:, c0:c0 + POOL_GROUP_DIM]
        mix = mix + jnp.dot(yg.astype(BF16), wout_ref[wout_row + c0:wout_row + c0 + POOL_GROUP_DIM, :],
                            preferred_element_type=F32)
    wout_row += POOL_WIDTH
    cw = convw_ref[...]
    conv = (zext_ref[HALO - 1:HALO - 1 + TM, :] * cw[0:1, :]
            + zext_ref[HALO:HALO + TM, :] * cw[1:2, :]
            + zext_ref[HALO + 1:HALO + 1 + TM, :] * cw[2:3, :])
    conv = r_ref[0, :, GB0:GB0 + CONV_WIDTH] * conv
    mix = mix + jnp.dot(conv.astype(BF16), wout_ref[wout_row:wout_row + CONV_WIDTH, :],
                        preferred_element_type=F32)

    xn = x_ref[0] + gate_ref[0] * mix
    xo_ref[0] = xn
    ms = jnp.mean(xn * xn, axis=-1, keepdims=True)
    h2 = (xn * lax.rsqrt(ms + EPS)) * g2_ref[...]
    h2 = h2 * (1.0 + scale_ref[0]) + shift_ref[0]
    h2_ref[0] = h2
    h_hi = h2.astype(BF16)
    h_lo = (h2 - h_hi.astype(F32)).astype(BF16)
    logits = (jnp.dot(h_hi, wr_ref[0], preferred_element_type=F32)
              + (jnp.dot(h_lo, wr_ref[0], preferred_element_type=F32)
                 + jnp.dot(h_hi, wr_ref[1], preferred_element_type=F32)))
    lane = lax.broadcasted_iota(jnp.int32, logits.shape, 1)
    logits = jnp.where(lane < N_EXPERTS, logits, -jnp.inf)
    e = jnp.exp(logits - jnp.max(logits, axis=-1, keepdims=True))
    aff_ref[0] = e / jnp.sum(e, axis=-1, keepdims=True)


def _mixer(att, r, x, wout, poolw, pscale, convw, gate, shift, scale, g2, wr, n_lat, n_ctx):
    B, S, D = x.shape
    n_lat_tiles = n_lat // TM
    halo_blocks = TM // HALO
    last_halo = S // HALO - 1
    row = lambda b, i: (b, i, 0)
    mod_idx = lambda b, i: (jnp.where(i < n_lat_tiles, b, B), 0, 0)
    const2 = lambda b, i: (0, 0)
    return pl.pallas_call(
        functools.partial(_mix_body, n_lat=n_lat, n_ctx=n_ctx),
        grid=(B, S // TM),
        in_specs=[
            pl.BlockSpec((1, TM, ATT_WIDTH), row),
            pl.BlockSpec((1, TM, REST_WIDTH), row),
            pl.BlockSpec((1, HALO, REST_WIDTH), lambda b, i: (b, jnp.maximum(i * halo_blocks - 1, 0), 0)),
            pl.BlockSpec((1, HALO, REST_WIDTH), lambda b, i: (b, jnp.minimum((i + 1) * halo_blocks, last_halo), 0)),
            pl.BlockSpec((1, TM, D), row),
            _resident((MIX_WIDTH, D), const2),
            _resident((len(POOL_WINDOWS), POOL_GROUP_DIM, POOL_GROUP_DIM), lambda b, i: (0, 0, 0)),
            _resident((1, POOL_WIDTH), const2),
            _resident((3, CONV_WIDTH), const2),
            pl.BlockSpec((1, 1, D), mod_idx),
            pl.BlockSpec((1, 1, D), mod_idx),
            pl.BlockSpec((1, 1, D), mod_idx),
            _resident((1, D), const2),
            _resident((2, D, ROUTER_LANES), lambda b, i: (0, 0, 0)),
        ],
        out_specs=[
            pl.BlockSpec((1, TM, D), row),
            pl.BlockSpec((1, TM, D), row),
            pl.BlockSpec((1, TM, ROUTER_LANES), row),
        ],
        out_shape=[
            jax.ShapeDtypeStruct((B, S, D), F32),
            jax.ShapeDtypeStruct((B, S, D), F32),
            jax.ShapeDtypeStruct((B, S, ROUTER_LANES), F32),
        ],
        scratch_shapes=[
            pltpu.VMEM((TM + 2 * HALO, POOL_WIDTH), F32),
            pltpu.VMEM((TM + 2 * HALO, CONV_WIDTH), F32),
        ],
        compiler_params=_cparams(2),
        name="mixer_out",
    )(att, r, r, r, x, wout, poolw, pscale, convw, gate, shift, scale, g2, wr)


def _moe_body(idx_ref, idx_next_ref, g_ref, gate_ref, gate_ctx_ref, wg_ref, wu_ref, wd_ref, h2_hbm, x_in_hbm,
              xo_hbm, tok_buf, xg_ref, acc_ref, sem_tok, sem_x, sem_out, *, n_slots, n_lat_slots, s_rows):
    del x_in_hbm
    e, b, f = pl.program_id(0), pl.program_id(1), pl.program_id(2)
    n_b, n_f = pl.num_programs(1), pl.num_programs(2)
    n_steps = pl.num_programs(0) * n_b
    t = e * n_b + b
    cur = t % 2
    nxt = 1 - cur
    b_next = (b + 1) % n_b

    def row_copy(hbm, buf, i_ref, sample, c):
        row = sample * s_rows + i_ref[0, 0, c]
        return hbm.at[pl.ds(row, 1)], buf.at[pl.ds(c, 1)]

    def gather_rows(hbm, buf, i_ref, sample, sem):
        def body(c, carry):
            src, dst = row_copy(hbm, buf, i_ref, sample, c)
            pltpu.make_async_copy(src, dst, sem).start()
            return carry
        lax.fori_loop(0, n_slots, body, 0, unroll=8)

    def scatter_rows(buf, hbm, i_ref, sample, sem):
        def body(c, carry):
            dst, src = row_copy(hbm, buf, i_ref, sample, c)
            pltpu.make_async_copy(src, dst, sem).start()
            return carry
        lax.fori_loop(0, n_slots, body, 0, unroll=8)

    def wait_gather(hbm, buf, sem):
        pltpu.make_async_copy(hbm.at[pl.ds(0, n_slots)], buf, sem).wait()

    def wait_scatter(buf, hbm, sem):
        pltpu.make_async_copy(buf, hbm.at[pl.ds(0, n_slots)], sem).wait()

    @pl.when(f == 0)
    def _():
        @pl.when(t == 0)
        def _():
            gather_rows(h2_hbm, tok_buf, idx_ref, b, sem_tok)
            gather_rows(xo_hbm, acc_ref.at[cur], idx_ref, b, sem_x.at[cur])

        wait_gather(h2_hbm, tok_buf, sem_tok)
        xg_ref[...] = tok_buf[...].astype(BF16)
        wait_gather(xo_hbm, acc_ref.at[cur], sem_x.at[cur])

        @pl.when(t + 1 < n_steps)
        def _():
            gather_rows(h2_hbm, tok_buf, idx_next_ref, b_next, sem_tok)

    xg = xg_ref[...]
    a = jnp.dot(xg, wg_ref[0].astype(BF16), preferred_element_type=F32)
    u = jnp.dot(xg, wu_ref[0].astype(BF16), preferred_element_type=F32)
    hm = (a * jax.nn.sigmoid(a) * u).astype(BF16)
    y = jnp.dot(hm, wd_ref[0].astype(BF16), preferred_element_type=F32)
    slot = lax.broadcasted_iota(jnp.int32, (n_slots, 1), 0)
    gate = jnp.where(slot < n_lat_slots, gate_ref[0], gate_ctx_ref[0])
    acc_ref[cur] += (y * g_ref[0]) * gate

    @pl.when((f == 1) & (t + 1 < n_steps))
    def _():
        @pl.when(t >= 1)
        def _():
            wait_scatter(acc_ref.at[nxt], xo_hbm, sem_out.at[nxt])

        gather_rows(xo_hbm, acc_ref.at[nxt], idx_next_ref, b_next, sem_x.at[nxt])

    @pl.when(f == n_f - 1)
    def _():
        scatter_rows(acc_ref.at[cur], xo_hbm, idx_ref, b, sem_out.at[cur])

        @pl.when(t == n_steps - 1)
        def _():
            wait_scatter(acc_ref.at[cur], xo_hbm, sem_out.at[cur])

            @pl.when(n_steps > 1)
            def _():
                wait_scatter(acc_ref.at[nxt], xo_hbm, sem_out.at[nxt])


def _moe(idx, g, gate, h2, x, w_gate, w_up, w_down, layer, n_lat_slots):
    T, _, C = idx.shape
    B = gate.shape[0] - 1
    E = T // B
    rows, D = x.shape
    step = lambda e, b, f: (e * B + b, 0, 0)
    next_step = lambda e, b, f: (jnp.minimum(e * B + b + 1, T - 1), 0, 0)
    return pl.pallas_call(
        functools.partial(_moe_body, n_slots=C, n_lat_slots=n_lat_slots, s_rows=rows // B),
        grid=(E, B, EXPERT_FF // FF_TILE),
        in_specs=[
            pl.BlockSpec((1, 1, C), step, memory_space=pltpu.SMEM),
            pl.BlockSpec((1, 1, C), next_step, memory_space=pltpu.SMEM),
            pl.BlockSpec((1, C, 1), step),
            pl.BlockSpec((1, 1, D), lambda e, b, f: (b, 0, 0)),
            pl.BlockSpec((1, 1, D), lambda e, b, f: (B, 0, 0)),
            pl.BlockSpec((None, 1, D, FF_TILE), lambda e, b, f: (layer, e, 0, f)),
            pl.BlockSpec((None, 1, D, FF_TILE), lambda e, b, f: (layer, e, 0, f)),
            pl.BlockSpec((None, 1, FF_TILE, D), lambda e, b, f: (layer, e, f, 0)),
            pl.BlockSpec(memory_space=pl.ANY),
            pl.BlockSpec(memory_space=pl.ANY),
        ],
        out_specs=pl.BlockSpec(memory_space=pl.ANY),
        out_shape=jax.ShapeDtypeStruct((rows, D), F32),
        input_output_aliases={9: 0},
        scratch_shapes=[
            pltpu.VMEM((C, D), F32),
            pltpu.VMEM((C, D), BF16),
            pltpu.VMEM((2, C, D), F32),
            pltpu.SemaphoreType.DMA,
            pltpu.SemaphoreType.DMA((2,)),
            pltpu.SemaphoreType.DMA((2,)),
        ],
        compiler_params=_cparams(3),
        name="moe",
    )(idx, idx, g, gate, gate, w_gate, w_up, w_down, h2, x)


def _final_body(x_ref, g_ref, o_ref):
    x = x_ref[0]
    o_ref[0] = (x * lax.rsqrt(jnp.mean(x * x, axis=-1, keepdims=True) + EPS)) * g_ref[...]


def _final_norm(x, g, n_lat):
    B, _, D = x.shape
    return pl.pallas_call(
        _final_body,
        grid=(B, n_lat // TM),
        in_specs=[pl.BlockSpec((1, TM, D), lambda b, i: (b, i, 0)),
                  pl.BlockSpec((1, D), lambda b, i: (0, 0))],
        out_specs=pl.BlockSpec((1, TM, D), lambda b, i: (b, i, 0)),
        out_shape=jax.ShapeDtypeStruct((B, n_lat, D), F32),
        compiler_params=_cparams(2),
        name="final_norm",
    )(x, g)


ADA_COLS = 1536


def _ada_body(s_ref, w_ref, b_ref, o_ref, *, n_rows):
    n_groups = w_ref.shape[0] // SUBLANES
    reps = ADA_COLS // LANES

    def body(kg, accs):
        w = w_ref[pl.ds(pl.multiple_of(kg * SUBLANES, SUBLANES), SUBLANES), :]
        return tuple(acc + jnp.tile(s_ref[kg, r], (1, reps)) * w for r, acc in enumerate(accs))

    accs = lax.fori_loop(0, n_groups, body, tuple(jnp.zeros((SUBLANES, ADA_COLS), F32) for _ in range(n_rows)),
                         unroll=4)
    for r, acc in enumerate(accs):
        o_ref[r:r + 1, :] = jnp.sum(acc, axis=0, keepdims=True) + b_ref[...]


def _ada_mod(sc_all, w_ada, b_ada):
    R, D = sc_all.shape
    L, _, W = w_ada.shape
    s_tiles = jnp.broadcast_to(sc_all.T.reshape(D // SUBLANES, SUBLANES, R).transpose(0, 2, 1)[..., None],
                               (D // SUBLANES, R, SUBLANES, LANES))
    return pl.pallas_call(
        functools.partial(_ada_body, n_rows=R),
        grid=(L, W // ADA_COLS),
        in_specs=[
            _resident((D // SUBLANES, R, SUBLANES, LANES), lambda l, j: (0, 0, 0, 0)),
            pl.BlockSpec((None, D, ADA_COLS), lambda l, j: (l, 0, j)),
            pl.BlockSpec((None, 1, ADA_COLS), lambda l, j: (l, 0, j)),
        ],
        out_specs=pl.BlockSpec((None, R, ADA_COLS), lambda l, j: (l, 0, j)),
        out_shape=jax.ShapeDtypeStruct((L, R, W), F32),
        compiler_params=_cparams(2),
        name="ada_mod",
    )(s_tiles, w_ada, b_ada.reshape(L, 1, W))


def _rope_tables(n_lat, s_tot):
    inv = 1.0 / (ROPE_BASE ** (jnp.arange(0, AXIS_ROT, 2, dtype=F32) / AXIS_ROT))
    pos = jnp.arange(n_lat)
    ang = jnp.concatenate([(pos // GRID_W)[:, None].astype(F32) * inv,
                           (pos % GRID_W)[:, None].astype(F32) * inv], axis=-1)
    cos, sin = jnp.cos(ang), jnp.sin(ang)
    cos_l = jnp.tile(cos, (1, LANES // AXIS_ROT))
    sin_l = jnp.tile(jnp.concatenate([-sin, sin], axis=-1), (1, LANES // ATT_HEAD_DIM))
    pad = s_tot - n_lat
    cos_l = jnp.concatenate([cos_l, jnp.ones((pad, LANES), F32)], axis=0)
    sin_l = jnp.concatenate([sin_l, jnp.zeros((pad, LANES), F32)], axis=0)
    return cos_l, sin_l


def kernel(x, c, ctx, c_ctx, w_ada, b_ada, norm1_g, norm2_g, w_in, w_out, lambda_q1, lambda_k1, lambda_q2,
           lambda_k2, subln_g, pool_w, pool_scale, conv_w, w_router, w_gate, w_up, w_down, final_g):
    B, n, D = x.shape
    n_ctx = ctx.shape[1]
    depth = w_in.shape[0]
    assert D == D_MODEL and n % TM == 0 and n % GRID_W == 0 and TQ == TM
    ctx_pad = -(-n_ctx // TM) * TM
    S = n + ctx_pad
    X = jnp.concatenate([x, ctx, jnp.zeros((B, ctx_pad - n_ctx, D), x.dtype)], axis=1)
    cos_l, sin_l = _rope_tables(n, S)
    sc_all = jax.nn.silu(jnp.concatenate([c, c_ctx[None]], axis=0))
    cap, cap_ctx = CAP_FACTOR * n // N_EXPERTS, CAP_FACTOR * n_ctx // N_EXPERTS
    wr_pad = jnp.pad(w_router, ((0, 0), (0, 0), (0, ROUTER_LANES - N_EXPERTS)))
    wr_hi = wr_pad.astype(BF16)
    wr_split = jnp.stack([wr_hi, (wr_pad - wr_hi.astype(F32)).astype(BF16)], axis=1)
    mod_all = _ada_mod(sc_all, w_ada, b_ada)

    for l in range(depth):
        lam_init = 0.8 - 0.6 * math.exp(-0.3 * l)
        mod = mod_all[l].reshape(B + 1, N_MOD, 1, D)
        qkv, r = _inproj(X, norm1_g[l][None], mod[:, 0], mod[:, 1], cos_l, sin_l, w_in[l].astype(BF16), n)
        att = _attention(qkv, lambda_q1[l][None], lambda_k1[l][None], lambda_q2[l][None], lambda_k2[l][None],
                         subln_g[l][None], n, n_ctx, lam_init)
        X, h2, aff = _mixer(att, r, X, w_out[l].astype(BF16), pool_w[l], pool_scale[l][None], conv_w[l],
                            mod[:, 2], mod[:, 3], mod[:, 4], norm2_g[l][None], wr_split[l], n, n_ctx)
        aff = aff[..., :N_EXPERTS]
        g_lat, i_lat = lax.top_k(aff[:, :n].transpose(0, 2, 1), cap)
        g_ctx, i_ctx = lax.top_k(aff[:, n:n + n_ctx].transpose(0, 2, 1), cap_ctx)
        C = cap + cap_ctx
        idx = jnp.concatenate([i_lat, i_ctx + n], axis=2).transpose(1, 0, 2).reshape(N_EXPERTS * B, 1, C)
        gsel = jnp.concatenate([g_lat, g_ctx], axis=2).transpose(1, 0, 2).reshape(N_EXPERTS * B, C, 1)
        X = _moe(idx, gsel, mod[:, 5], h2.reshape(B * S, D), X.reshape(B * S, D), w_gate, w_up, w_down, l,
                 cap).reshape(B, S, D)
    return _final_norm(X, final_g[None], n)
```

```python
import functools
import math

import jax
import jax.numpy as jnp
from jax import lax
from jax.experimental import pallas as pl
from jax.experimental.pallas import tpu as pltpu

F32 = jnp.float32
BF16 = jnp.bfloat16

D_MODEL = 2048
N_MOD = 6
GRID_W = 64
ATT_HEADS = 8
ATT_HEAD_DIM = 64
ATT_V_DIM = 2 * ATT_HEAD_DIM
QK_WIDTH = ATT_HEADS * 2 * ATT_HEAD_DIM
ATT_WIDTH = ATT_HEADS * ATT_V_DIM
ROPE_BASE = 10000.0
AXIS_ROT = ATT_HEAD_DIM // 2
POOL_WINDOWS = (2, 4, 8, 16)
POOL_WIDTH = D_MODEL // 4
POOL_GROUP_DIM = POOL_WIDTH // len(POOL_WINDOWS)
CONV_WIDTH = D_MODEL // 4
MIX_WIDTH = ATT_WIDTH + POOL_WIDTH + CONV_WIDTH
REST_WIDTH = POOL_WIDTH + 3 * CONV_WIDTH
IN_WIDTH = 2 * QK_WIDTH + ATT_WIDTH + REST_WIDTH
N_EXPERTS = 16
EXPERT_FF = 1024
CAP_FACTOR = 2
EPS = 1e-6
QK_SCALE = ATT_HEAD_DIM ** -0.5 * math.log2(math.e)

LANES = 128
SUBLANES = 8
HALO = SUBLANES
TM = 512
TQ = 512
N_CHUNK = 512
FF_TILE = 256
ROUTER_LANES = LANES
VMEM_LIMIT = 56 * 1024 * 1024


def _cparams(n_grid_dims):
    return pltpu.CompilerParams(
        dimension_semantics=("arbitrary",) * n_grid_dims,
        vmem_limit_bytes=VMEM_LIMIT,
    )


def _resident(block_shape, index_map):
    return pl.BlockSpec(block_shape, index_map, pipeline_mode=pl.Buffered(1))


def _inproj_body(x_ref, g_ref, shift_ref, scale_ref, cos_ref, sin_ref, w_ref, qkv_ref, r_ref):
    x = x_ref[0]
    ms = jnp.mean(x * x, axis=-1, keepdims=True)
    h = (x * lax.rsqrt(ms + EPS)) * g_ref[...]
    h = h * (1.0 + scale_ref[0]) + shift_ref[0]
    hb = h.astype(BF16)
    cos = cos_ref[...]
    sin = sin_ref[...]
    lane = lax.broadcasted_iota(jnp.int32, cos.shape, 1)
    first_half = (lane % ATT_HEAD_DIM) < AXIS_ROT
    n_qk = 2 * QK_WIDTH // N_CHUNK
    n_qkv = n_qk + ATT_WIDTH // N_CHUNK
    for c in range(IN_WIDTH // N_CHUNK):
        u = jnp.dot(hb, w_ref[:, c * N_CHUNK:(c + 1) * N_CHUNK], preferred_element_type=F32)
        if c < n_qk:
            for s in range(N_CHUNK // LANES):
                us = u[:, s * LANES:(s + 1) * LANES]
                partner = jnp.where(first_half, pltpu.roll(us, LANES - AXIS_ROT, 1), pltpu.roll(us, AXIS_ROT, 1))
                o = us * cos + partner * sin
                if c < n_qk // 2:
                    o = o * QK_SCALE
                col = c * N_CHUNK + s * LANES
                qkv_ref[0, :, col:col + LANES] = o.astype(BF16)
        elif c < n_qkv:
            qkv_ref[0, :, c * N_CHUNK:(c + 1) * N_CHUNK] = u.astype(BF16)
        else:
            col = (c - n_qkv) * N_CHUNK
            r_ref[0, :, col:col + N_CHUNK] = u


def _inproj(x, g, shift, scale, cos, sin, w, n_lat):
    B, S, D = x.shape
    n_lat_tiles = n_lat // TM
    mod_idx = lambda b, i: (jnp.where(i < n_lat_tiles, b, B), 0, 0)
    return pl.pallas_call(
        _inproj_body,
        grid=(B, S // TM),
        in_specs=[
            pl.BlockSpec((1, TM, D), lambda b, i: (b, i, 0)),
            _resident((1, D), lambda b, i: (0, 0)),
            pl.BlockSpec((1, 1, D), mod_idx),
            pl.BlockSpec((1, 1, D), mod_idx),
            pl.BlockSpec((TM, LANES), lambda b, i: (i, 0)),
            pl.BlockSpec((TM, LANES), lambda b, i: (i, 0)),
            _resident((D, IN_WIDTH), lambda b, i: (0, 0)),
        ],
        out_specs=[
            pl.BlockSpec((1, TM, 2 * QK_WIDTH + ATT_WIDTH), lambda b, i: (b, i, 0)),
            pl.BlockSpec((1, TM, REST_WIDTH), lambda b, i: (b, i, 0)),
        ],
        out_shape=[
            jax.ShapeDtypeStruct((B, S, 2 * QK_WIDTH + ATT_WIDTH), BF16),
            jax.ShapeDtypeStruct((B, S, REST_WIDTH), F32),
        ],
        compiler_params=_cparams(2),
        name="inproj",
    )(x, g, shift, scale, cos, sin, w)


def _attn_body(lq1_ref, lk1_ref, lq2_ref, lk2_ref, sg_ref, q_ref, k_ref, v_ref, o_ref,
               qs_ref, acc_ref, s_a, s_b, p_a, p_b, *, n_lat, n_ctx, tk, lam_init):
    qi = pl.program_id(2)

    def attend(key_start, kc_len, n_chunks):
        assert n_chunks % 2 == 1
        q = q_ref[0]
        lane = lax.broadcasted_iota(jnp.int32, q.shape, 1)
        zero = jnp.zeros_like(q)
        qs_ref[:TQ, :] = jnp.where(lane < ATT_HEAD_DIM, q, zero)
        qs_ref[TQ:, :] = jnp.where(lane >= ATT_HEAD_DIM, q, zero)
        acc_ref[...] = jnp.zeros_like(acc_ref)
        ones = jnp.ones((kc_len, LANES), BF16)

        def keys_at(j):
            return pl.ds(pl.multiple_of(key_start + j * kc_len, LANES), kc_len)

        def scores(j, s_ref):
            s = lax.dot_general(qs_ref[...], k_ref[0, keys_at(j), :], (((1,), (1,)), ((), ())),
                                preferred_element_type=F32)
            s_ref[:, :kc_len] = s
            return jnp.max(s, axis=1, keepdims=True)

        def softmax(s_ref, p_ref, row_max, m):
            m_new = jnp.maximum(m, row_max)
            p_ref[:, :kc_len] = jnp.exp2(s_ref[:, :kc_len] - m_new).astype(BF16)
            return m_new, jnp.exp2(m - m_new)

        def values(j, p_ref, alpha):
            vext = jnp.concatenate([v_ref[0, keys_at(j), :], ones], axis=1)
            acc_ref[...] = alpha * acc_ref[...] + jnp.dot(p_ref[:, :kc_len], vext, preferred_element_type=F32)

        m = jnp.full((2 * TQ, 1), -jnp.inf, F32)
        mx_a = scores(0, s_a)
        if n_chunks == 1:
            m, al_a = softmax(s_a, p_a, mx_a, m)
            values(0, p_a, al_a)
        else:
            mx_b = scores(1, s_b)
            m, al_a = softmax(s_a, p_a, mx_a, m)

            def pair(i, carry):
                m, mx_b, al_a = carry
                mx_a = scores(2 * i + 2, s_a)
                m, al_b = softmax(s_b, p_b, mx_b, m)
                values(2 * i, p_a, al_a)
                mx_b = scores(2 * i + 3, s_b)
                m, al_a = softmax(s_a, p_a, mx_a, m)
                values(2 * i + 1, p_b, al_b)
                return m, mx_b, al_a

            carry = (m, mx_b, al_a)
            for i in range(n_chunks // 2 - 1):
                carry = pair(i, carry)
            m, mx_b, al_a = carry
            last = n_chunks - 1
            mx_a = scores(last, s_a)
            m, al_b = softmax(s_b, p_b, mx_b, m)
            values(last - 2, p_a, al_a)
            m, al_a = softmax(s_a, p_a, mx_a, m)
            values(last - 1, p_b, al_b)
            values(last, p_a, al_a)

        a0 = acc_ref[:TQ, :LANES] / acc_ref[:TQ, LANES:LANES + 1]
        a1 = acc_ref[TQ:, :LANES] / acc_ref[TQ:, LANES:LANES + 1]
        lam = (jnp.exp(jnp.sum(lq1_ref[...] * lk1_ref[...], axis=1, keepdims=True))
               - jnp.exp(jnp.sum(lq2_ref[...] * lk2_ref[...], axis=1, keepdims=True)) + lam_init)
        o = a0 - lam * a1
        y = o * lax.rsqrt(jnp.mean(o * o, axis=-1, keepdims=True) + EPS)
        o_ref[0] = ((y * sg_ref[...]) * (1.0 - lam_init)).astype(o_ref.dtype)

    @pl.when(qi < n_lat // TQ)
    def _():
        attend(0, tk, (n_lat + n_ctx) // tk)

    @pl.when(qi >= n_lat // TQ)
    def _():
        attend(n_lat, n_ctx, 1)


def _attention(qkv, lq1, lk1, lq2, lk2, sg, n_lat, n_ctx, lam_init):
    B, S, _ = qkv.shape
    n_keys = n_lat + n_ctx
    tk = 768 if n_keys % 768 == 0 else 256
    assert n_keys % tk == 0 and (n_keys // tk) % 2 == 1 and n_ctx <= tk
    small = lambda b, h, i: (0, 0)
    return pl.pallas_call(
        functools.partial(_attn_body, n_lat=n_lat, n_ctx=n_ctx, tk=tk, lam_init=lam_init),
        grid=(B, ATT_HEADS, S // TQ),
        in_specs=[
            pl.BlockSpec((1, ATT_HEAD_DIM), small),
            pl.BlockSpec((1, ATT_HEAD_DIM), small),
            pl.BlockSpec((1, ATT_HEAD_DIM), small),
            pl.BlockSpec((1, ATT_HEAD_DIM), small),
            pl.BlockSpec((1, ATT_V_DIM), small),
            pl.BlockSpec((1, TQ, LANES), lambda b, h, i: (b, i, h)),
            pl.BlockSpec((1, S, LANES), lambda b, h, i: (b, 0, ATT_HEADS + h)),
            pl.BlockSpec((1, S, LANES), lambda b, h, i: (b, 0, 2 * ATT_HEADS + h)),
        ],
        out_specs=pl.BlockSpec((1, TQ, LANES), lambda b, h, i: (b, i, h)),
        out_shape=jax.ShapeDtypeStruct((B, S, ATT_WIDTH), BF16),
        scratch_shapes=[
            pltpu.VMEM((2 * TQ, LANES), BF16),
            pltpu.VMEM((2 * TQ, 2 * LANES), F32),
            pltpu.VMEM((2 * TQ, tk), F32),
            pltpu.VMEM((2 * TQ, tk), F32),
            pltpu.VMEM((2 * TQ, tk), BF16),
            pltpu.VMEM((2 * TQ, tk), BF16),
        ],
        compiler_params=_cparams(3),
        name="diff_attn",
    )(lq1, lk1, lq2, lk2, sg, qkv, qkv, qkv)


def _mix_body(att_ref, r_ref, rprev_ref, rnext_ref, x_ref, wout_ref, poolw_ref, pscale_ref, convw_ref,
              gate_ref, shift_ref, scale_ref, g2_ref, wr_ref,
              xo_ref, h2_ref, aff_ref, pext_ref, zext_ref, ycat_ref, *, n_lat, n_ctx):
    i = pl.program_id(1)
    is_ctx = i >= n_lat // TM
    seq_lo = jnp.where(is_ctx, n_lat, 0)
    seq_hi = jnp.where(is_ctx, n_lat + n_ctx, n_lat)
    row0 = i * TM

    def in_seq(first_row, n_rows):
        rows = first_row + lax.broadcasted_iota(jnp.int32, (n_rows, 1), 0)
        return (rows >= seq_lo) & (rows < seq_hi)

    P0, GB0, GC0, XI0 = 0, POOL_WIDTH, POOL_WIDTH + CONV_WIDTH, POOL_WIDTH + 2 * CONV_WIDTH

    def fill(dst_row, src_ref, first_row, n_rows):
        ok = in_seq(first_row, n_rows)
        src = src_ref[0]
        pext_ref[dst_row:dst_row + n_rows, :] = jnp.where(ok, src[:, P0:P0 + POOL_WIDTH], 0.0)
        z = src[:, GC0:GC0 + CONV_WIDTH] * src[:, XI0:XI0 + CONV_WIDTH]
        zext_ref[dst_row:dst_row + n_rows, :] = jnp.where(ok, z, 0.0)

    fill(0, rprev_ref, row0 - HALO, HALO)
    fill(HALO, r_ref, row0, TM)
    fill(HALO + TM, rnext_ref, row0 + TM, HALO)

    t = row0 + lax.broadcasted_iota(jnp.int32, (TM, 1), 0)
    ycat_ref[:, 0:ATT_WIDTH] = att_ref[0]
    for g, w in enumerate(POOL_WINDOWS):
        c0 = g * POOL_GROUP_DIM
        back, fwd = w // 2, w - w // 2
        tot = pext_ref[HALO - back:HALO - back + TM, c0:c0 + POOL_GROUP_DIM]
        for j in range(-back + 1, fwd):
            tot = tot + pext_ref[HALO + j:HALO + j + TM, c0:c0 + POOL_GROUP_DIM]
        cnt = jnp.minimum(t + fwd, seq_hi) - jnp.maximum(t - back, seq_lo)
        cnt = jnp.maximum(cnt, 1).astype(F32)
        f = tot / cnt - pext_ref[HALO:HALO + TM, c0:c0 + POOL_GROUP_DIM]
        yg = jnp.dot(f.astype(BF16), poolw_ref[g].astype(BF16), preferred_element_type=F32)
        yg = yg * pscale_ref[:, c0:c0 + POOL_GROUP_DIM]
        ycat_ref[:, ATT_WIDTH + c0:ATT_WIDTH + c0 + POOL_GROUP_DIM] = yg.astype(BF16)
    cw = convw_ref[...]
    conv = (zext_ref[HALO - 1:HALO - 1 + TM, :] * cw[0:1, :]
            + zext_ref[HALO:HALO + TM, :] * cw[1:2, :]
            + zext_ref[HALO + 1:HALO + 1 + TM, :] * cw[2:3, :])
    conv = r_ref[0, :, GB0:GB0 + CONV_WIDTH] * conv
    ycat_ref[:, ATT_WIDTH + POOL_WIDTH:MIX_WIDTH] = conv.astype(BF16)
    mix = jnp.dot(ycat_ref[...], wout_ref[...], preferred_element_type=F32)

    xn = x_ref[0] + gate_ref[0] * mix
    xo_ref[0] = xn
    ms = jnp.mean(xn * xn, axis=-1, keepdims=True)
    h2 = (xn * lax.rsqrt(ms + EPS)) * g2_ref[...]
    h2 = h2 * (1.0 + scale_ref[0]) + shift_ref[0]
    h2_ref[0] = h2
    h_hi = h2.astype(BF16)
    h_lo = (h2 - h_hi.astype(F32)).astype(BF16)
    logits = (jnp.dot(h_hi, wr_ref[0], preferred_element_type=F32)
              + (jnp.dot(h_lo, wr_ref[0], preferred_element_type=F32)
                 + jnp.dot(h_hi, wr_ref[1], preferred_element_type=F32)))
    lane = lax.broadcasted_iota(jnp.int32, logits.shape, 1)
    logits = jnp.where(lane < N_EXPERTS, logits, -jnp.inf)
    e = jnp.exp(logits - jnp.max(logits, axis=-1, keepdims=True))
    aff_ref[0] = e / jnp.sum(e, axis=-1, keepdims=True)


def _mixer(att, r, x, wout, poolw, pscale, convw, gate, shift, scale, g2, wr, n_lat, n_ctx):
    B, S, D = x.shape
    n_lat_tiles = n_lat // TM
    halo_blocks = TM // HALO
    last_halo = S // HALO - 1
    row = lambda b, i: (b, i, 0)
    mod_idx = lambda b, i: (jnp.where(i < n_lat_tiles, b, B), 0, 0)
    const2 = lambda b, i: (0, 0)
    return pl.pallas_call(
        functools.partial(_mix_body, n_lat=n_lat, n_ctx=n_ctx),
        grid=(B, S // TM),
        in_specs=[
            pl.BlockSpec((1, TM, ATT_WIDTH), row),
            pl.BlockSpec((1, TM, REST_WIDTH), row),
            pl.BlockSpec((1, HALO, REST_WIDTH), lambda b, i: (b, jnp.maximum(i * halo_blocks - 1, 0), 0)),
            pl.BlockSpec((1, HALO, REST_WIDTH), lambda b, i: (b, jnp.minimum((i + 1) * halo_blocks, last_halo), 0)),
            pl.BlockSpec((1, TM, D), row),
            _resident((MIX_WIDTH, D), const2),
            _resident((len(POOL_WINDOWS), POOL_GROUP_DIM, POOL_GROUP_DIM), lambda b, i: (0, 0, 0)),
            _resident((1, POOL_WIDTH), const2),
            _resident((3, CONV_WIDTH), const2),
            pl.BlockSpec((1, 1, D), mod_idx),
            pl.BlockSpec((1, 1, D), mod_idx),
            pl.BlockSpec((1, 1, D), mod_idx),
            _resident((1, D), const2),
            _resident((2, D, ROUTER_LANES), lambda b, i: (0, 0, 0)),
        ],
        out_specs=[
            pl.BlockSpec((1, TM, D), row),
            pl.BlockSpec((1, TM, D), row),
            pl.BlockSpec((1, TM, ROUTER_LANES), row),
        ],
        out_shape=[
            jax.ShapeDtypeStruct((B, S, D), F32),
            jax.ShapeDtypeStruct((B, S, D), F32),
            jax.ShapeDtypeStruct((B, S, ROUTER_LANES), F32),
        ],
        scratch_shapes=[
            pltpu.VMEM((TM + 2 * HALO, POOL_WIDTH), F32),
            pltpu.VMEM((TM + 2 * HALO, CONV_WIDTH), F32),
            pltpu.VMEM((TM, MIX_WIDTH), BF16),
        ],
        compiler_params=_cparams(2),
        name="mixer_out",
    )(att, r, r, r, x, wout, poolw, pscale, convw, gate, shift, scale, g2, wr)


def _moe_body(idx_ref, idx_next_ref, g_ref, gate_ref, gate_ctx_ref, wg_ref, wu_ref, wd_ref, h2_hbm, x_in_hbm,
              xo_hbm, tok_buf, xg_ref, acc_a, acc_b, sem_tok, sem_x, sem_out, *, n_slots, n_lat_slots):
    del x_in_hbm
    e, b, f = pl.program_id(0), pl.program_id(1), pl.program_id(2)
    n_b, n_f = pl.num_programs(1), pl.num_programs(2)
    n_steps = pl.num_programs(0) * n_b
    t = e * n_b + b
    n_grp = n_slots // SUBLANES
    grp_per_tile = n_grp // 2
    D = xg_ref.shape[1]

    def row_pair(hbm, buf, i_ref, grp, j):
        row = i_ref[0, 0, grp * SUBLANES + j]
        return hbm.at[row >> 3, pl.ds(row & (SUBLANES - 1), 1)], buf.at[grp, pl.ds(j, 1)]

    def start_gather(hbm, buf, i_ref, grp, sem):
        for j in range(SUBLANES):
            src, dst = row_pair(hbm, buf, i_ref, grp, j)
            pltpu.make_async_copy(src, dst, sem).start()

    def gather_all(hbm, buf, i_ref, sem):
        def body(grp, carry):
            start_gather(hbm, buf, i_ref, grp, sem)
            return carry
        lax.fori_loop(0, n_grp, body, 0)

    def scatter_all(buf, hbm, i_ref, sem):
        def body(grp, carry):
            for j in range(SUBLANES):
                dst, src = row_pair(hbm, buf, i_ref, grp, j)
                pltpu.make_async_copy(src, dst, sem).start()
            return carry
        lax.fori_loop(0, n_grp, body, 0)

    def gather_part(hbm, buf, i_ref, part, sem):
        for k in range(grp_per_tile):
            start_gather(hbm, buf, i_ref, part * grp_per_tile + k, sem)

    def wait_gather(hbm, buf, sem):
        pltpu.make_async_copy(hbm.at[pl.ds(0, n_grp)], buf, sem).wait()

    def wait_scatter(buf, hbm, sem):
        pltpu.make_async_copy(buf, hbm.at[pl.ds(0, n_grp)], sem).wait()

    def ffn_tile(acc):
        xg = xg_ref[...]
        a = jnp.dot(xg, wg_ref[0].astype(BF16), preferred_element_type=F32)
        u = jnp.dot(xg, wu_ref[0].astype(BF16), preferred_element_type=F32)
        hm = (a * jax.nn.sigmoid(a) * u).astype(BF16)
        y = jnp.dot(hm, wd_ref[0].astype(BF16), preferred_element_type=F32)
        slot = lax.broadcasted_iota(jnp.int32, (n_slots, 1), 0)
        gate = jnp.where(slot < n_lat_slots, gate_ref[0], gate_ctx_ref[0])
        acc[...] += ((y * g_ref[0]) * gate).reshape(n_grp, SUBLANES, D)

    def step(acc, acc_next, cur, nxt):
        @pl.when(f == 0)
        def _():
            @pl.when(t == 0)
            def _():
                gather_all(h2_hbm, tok_buf, idx_ref, sem_tok)
                gather_all(xo_hbm, acc, idx_ref, sem_x.at[cur])

            wait_gather(h2_hbm, tok_buf, sem_tok)
            xg_ref[...] = tok_buf[...].reshape(n_slots, D).astype(BF16)
            wait_gather(xo_hbm, acc, sem_x.at[cur])

        @pl.when(f < 2)
        def _():
            ffn_tile(acc)
            gather_part(h2_hbm, tok_buf, idx_next_ref, f, sem_tok)

        @pl.when((f == 2) & (t >= 1))
        def _():
            wait_scatter(acc_next, xo_hbm, sem_out.at[nxt])

        @pl.when(f >= 2)
        def _():
            ffn_tile(acc)
            gather_part(xo_hbm, acc_next, idx_next_ref, f - 2, sem_x.at[nxt])

        @pl.when(f == n_f - 1)
        def _():
            scatter_all(acc, xo_hbm, idx_ref, sem_out.at[cur])

            @pl.when(t == n_steps - 1)
            def _():
                wait_gather(h2_hbm, tok_buf, sem_tok)
                wait_gather(xo_hbm, acc_next, sem_x.at[nxt])
                wait_scatter(acc, xo_hbm, sem_out.at[cur])

    @pl.when(t % 2 == 0)
    def _():
        step(acc_a, acc_b, 0, 1)

    @pl.when(t % 2 == 1)
    def _():
        step(acc_b, acc_a, 1, 0)


def _moe(rows_idx, g, gate, h2, x, w_gate, w_up, w_down, layer, n_lat_slots):
    T, _, C = rows_idx.shape
    B = gate.shape[0] - 1
    E = T // B
    rows, D = x.shape
    n_f = EXPERT_FF // FF_TILE
    assert B == 2 and n_f == 4 and C % (2 * SUBLANES) == 0 and rows % SUBLANES == 0
    step = lambda e, b, f: (e * B + b, 0, 0)
    next_step = lambda e, b, f: ((e * B + b + 1) % T, 0, 0)
    row_tiles = (rows // SUBLANES, SUBLANES, D)
    buf = (C // SUBLANES, SUBLANES, D)
    return pl.pallas_call(
        functools.partial(_moe_body, n_slots=C, n_lat_slots=n_lat_slots),
        grid=(E, B, n_f),
        in_specs=[
            pl.BlockSpec((1, 1, C), step, memory_space=pltpu.SMEM),
            pl.BlockSpec((1, 1, C), next_step, memory_space=pltpu.SMEM),
            pl.BlockSpec((1, C, 1), step),
            pl.BlockSpec((1, 1, D), lambda e, b, f: (b, 0, 0)),
            pl.BlockSpec((1, 1, D), lambda e, b, f: (B, 0, 0)),
            pl.BlockSpec((None, 1, D, FF_TILE), lambda e, b, f: (layer, e, 0, f)),
            pl.BlockSpec((None, 1, D, FF_TILE), lambda e, b, f: (layer, e, 0, f)),
            pl.BlockSpec((None, 1, FF_TILE, D), lambda e, b, f: (layer, e, f, 0)),
            pl.BlockSpec(memory_space=pl.ANY),
            pl.BlockSpec(memory_space=pl.ANY),
        ],
        out_specs=pl.BlockSpec(memory_space=pl.ANY),
        out_shape=jax.ShapeDtypeStruct(row_tiles, F32),
        input_output_aliases={9: 0},
        scratch_shapes=[
            pltpu.VMEM(buf, F32),
            pltpu.VMEM((C, D), BF16),
            pltpu.VMEM(buf, F32),
            pltpu.VMEM(buf, F32),
            pltpu.SemaphoreType.DMA,
            pltpu.SemaphoreType.DMA((2,)),
            pltpu.SemaphoreType.DMA((2,)),
        ],
        compiler_params=_cparams(3),
        name="moe",
    )(rows_idx, rows_idx, g, gate, gate, w_gate, w_up, w_down, h2.reshape(row_tiles), x.reshape(row_tiles)
      ).reshape(rows, D)


def _final_body(x_ref, g_ref, o_ref):
    x = x_ref[0]
    o_ref[0] = (x * lax.rsqrt(jnp.mean(x * x, axis=-1, keepdims=True) + EPS)) * g_ref[...]


def _final_norm(x, g, n_lat):
    B, _, D = x.shape
    return pl.pallas_call(
        _final_body,
        grid=(B, n_lat // TM),
        in_specs=[pl.BlockSpec((1, TM, D), lambda b, i: (b, i, 0)),
                  pl.BlockSpec((1, D), lambda b, i: (0, 0))],
        out_specs=pl.BlockSpec((1, TM, D), lambda b, i: (b, i, 0)),
        out_shape=jax.ShapeDtypeStruct((B, n_lat, D), F32),
        compiler_params=_cparams(2),
        name="final_norm",
    )(x, g)


ADA_COLS = 1536


def _ada_body(s_ref, w_ref, b_ref, o_ref, *, n_rows):
    n_groups = w_ref.shape[0] // SUBLANES
    reps = ADA_COLS // LANES

    def body(kg, accs):
        w = w_ref[pl.ds(pl.multiple_of(kg * SUBLANES, SUBLANES), SUBLANES), :]
        return tuple(acc + jnp.tile(s_ref[kg, r], (1, reps)) * w for r, acc in enumerate(accs))

    accs = lax.fori_loop(0, n_groups, body, tuple(jnp.zeros((SUBLANES, ADA_COLS), F32) for _ in range(n_rows)),
                         unroll=4)
    for r, acc in enumerate(accs):
        o_ref[r:r + 1, :] = jnp.sum(acc, axis=0, keepdims=True) + b_ref[...]


def _ada_mod(sc_all, w_ada, b_ada):
    R, D = sc_all.shape
    L, _, W = w_ada.shape
    s_tiles = jnp.broadcast_to(sc_all.T.reshape(D // SUBLANES, SUBLANES, R).transpose(0, 2, 1)[..., None],
                               (D // SUBLANES, R, SUBLANES, LANES))
    return pl.pallas_call(
        functools.partial(_ada_body, n_rows=R),
        grid=(L, W // ADA_COLS),
        in_specs=[
            _resident((D // SUBLANES, R, SUBLANES, LANES), lambda l, j: (0, 0, 0, 0)),
            pl.BlockSpec((None, D, ADA_COLS), lambda l, j: (l, 0, j)),
            pl.BlockSpec((None, 1, ADA_COLS), lambda l, j: (l, 0, j)),
        ],
        out_specs=pl.BlockSpec((None, R, ADA_COLS), lambda l, j: (l, 0, j)),
        out_shape=jax.ShapeDtypeStruct((L, R, W), F32),
        compiler_params=_cparams(2),
        name="ada_mod",
    )(s_tiles, w_ada, b_ada.reshape(L, 1, W))


def _rope_tables(n_lat, s_tot):
    inv = 1.0 / (ROPE_BASE ** (jnp.arange(0, AXIS_ROT, 2, dtype=F32) / AXIS_ROT))
    pos = jnp.arange(n_lat)
    ang = jnp.concatenate([(pos // GRID_W)[:, None].astype(F32) * inv,
                           (pos % GRID_W)[:, None].astype(F32) * inv], axis=-1)
    cos, sin = jnp.cos(ang), jnp.sin(ang)
    cos_l = jnp.tile(cos, (1, LANES // AXIS_ROT))
    sin_l = jnp.tile(jnp.concatenate([-sin, sin], axis=-1), (1, LANES // ATT_HEAD_DIM))
    pad = s_tot - n_lat
    cos_l = jnp.concatenate([cos_l, jnp.ones((pad, LANES), F32)], axis=0)
    sin_l = jnp.concatenate([sin_l, jnp.zeros((pad, LANES), F32)], axis=0)
    return cos_l, sin_l


def kernel(x, c, ctx, c_ctx, w_ada, b_ada, norm1_g, norm2_g, w_in, w_out, lambda_q1, lambda_k1, lambda_q2,
           lambda_k2, subln_g, pool_w, pool_scale, conv_w, w_router, w_gate, w_up, w_down, final_g):
    B, n, D = x.shape
    n_ctx = ctx.shape[1]
    depth = w_in.shape[0]
    assert D == D_MODEL and n % TM == 0 and n % GRID_W == 0 and TQ == TM
    ctx_pad = -(-n_ctx // TM) * TM
    S = n + ctx_pad
    X = jnp.concatenate([x, ctx, jnp.zeros((B, ctx_pad - n_ctx, D), x.dtype)], axis=1)
    cos_l, sin_l = _rope_tables(n, S)
    sc_all = jax.nn.silu(jnp.concatenate([c, c_ctx[None]], axis=0))
    cap, cap_ctx = CAP_FACTOR * n // N_EXPERTS, CAP_FACTOR * n_ctx // N_EXPERTS
    wr_pad = jnp.pad(w_router, ((0, 0), (0, 0), (0, ROUTER_LANES - N_EXPERTS)))
    wr_hi = wr_pad.astype(BF16)
    wr_split = jnp.stack([wr_hi, (wr_pad - wr_hi.astype(F32)).astype(BF16)], axis=1)
    mod_all = _ada_mod(sc_all, w_ada, b_ada)

    for l in range(depth):
        lam_init = 0.8 - 0.6 * math.exp(-0.3 * l)
        mod = mod_all[l].reshape(B + 1, N_MOD, 1, D)
        qkv, r = _inproj(X, norm1_g[l][None], mod[:, 0], mod[:, 1], cos_l, sin_l, w_in[l].astype(BF16), n)
        att = _attention(qkv, lambda_q1[l][None], lambda_k1[l][None], lambda_q2[l][None], lambda_k2[l][None],
                         subln_g[l][None], n, n_ctx, lam_init)
        X, h2, aff = _mixer(att, r, X, w_out[l].astype(BF16), pool_w[l], pool_scale[l][None], conv_w[l],
                            mod[:, 2], mod[:, 3], mod[:, 4], norm2_g[l][None], wr_split[l], n, n_ctx)
        aff = aff[..., :N_EXPERTS]
        g_lat, i_lat = lax.top_k(aff[:, :n].transpose(0, 2, 1), cap)
        g_ctx, i_ctx = lax.top_k(aff[:, n:n + n_ctx].transpose(0, 2, 1), cap_ctx)
        C = cap + cap_ctx
        rows_idx = jnp.concatenate([i_lat, i_ctx + n], axis=2) + (jnp.arange(B, dtype=jnp.int32) * S)[:, None, None]
        rows_idx = rows_idx.transpose(1, 0, 2).reshape(N_EXPERTS * B, 1, C)
        gsel = jnp.concatenate([g_lat, g_ctx], axis=2).transpose(1, 0, 2).reshape(N_EXPERTS * B, C, 1)
        X = _moe(rows_idx, gsel, mod[:, 5], h2.reshape(B * S, D), X.reshape(B * S, D), w_gate, w_up, w_down, l,
                 cap).reshape(B, S, D)
    return _final_norm(X, final_g[None], n)
```
